```python
import jax, jax.numpy as jnp
from jax import lax
import numpy as np

D_MODEL = 4096
BATCH = 2
SEQ = 8192
DEPTH = 1

MIX_WIDTH = D_MODEL
RWKV_WIDTH = MIX_WIDTH // 2
CONV_WIDTH = MIX_WIDTH - RWKV_WIDTH
HEAD_DIM = 64
N_RWKV_HEADS = RWKV_WIDTH // HEAD_DIM
N_CONV_GROUPS = CONV_WIDTH // HEAD_DIM
DECAY_LORA = 96
A_LORA = 96
GATE_LORA = 256
CONV_K = 31
D_FF = 4 * D_MODEL
N_SHIFT_COLS = 3 * RWKV_WIDTH + DECAY_LORA + A_LORA + GATE_LORA
N_IN_COLS = N_SHIFT_COLS + 2 * CONV_WIDTH
RWKV_SPLITS = (RWKV_WIDTH, 2 * RWKV_WIDTH, 3 * RWKV_WIDTH,
               3 * RWKV_WIDTH + DECAY_LORA, 3 * RWKV_WIDTH + DECAY_LORA + A_LORA)
RMS_EPS = 1e-6
LN_EPS = 1e-5
GN_EPS = 64e-5
L2_EPS = 1e-12

kernel_name = "hymba_rwkv7_conformer_adaln_block"


def rmsnorm(x, g):
    xf = x.astype(jnp.float32)
    y = xf * lax.rsqrt(jnp.mean(xf * xf, axis=-1, keepdims=True) + RMS_EPS)
    return (y * g.astype(jnp.float32)).astype(x.dtype)


def token_shift(p, mu):
    p_prev = jnp.pad(p, ((0, 0), (1, 0), (0, 0)))[:, :-1]
    return p + (p_prev - p) * mu


def rwkv7_time_mix(p, w0, w_decay_up, a0, w_a_up, w_g_up, k_k, k_a, r_k, lnx_g, lnx_b):
    out_dtype = p.dtype
    p = p.astype(jnp.float32)
    B, T, _ = p.shape
    H, N = N_RWKV_HEADS, HEAD_DIM
    r, k, v, dw, da, dg = jnp.split(p, RWKV_SPLITS, axis=-1)
    w = -jax.nn.softplus(-(w0 + jnp.tanh(dw) @ w_decay_up)) - 0.5
    decay = jnp.exp(-jnp.exp(w))
    a = jax.nn.sigmoid(a0 + da @ w_a_up)
    g = jax.nn.sigmoid(dg) @ w_g_up
    heads = lambda t: t.reshape(B, T, H, N)
    kk = heads(k * k_k)
    kk = kk / jnp.maximum(jnp.linalg.norm(kk, axis=-1, keepdims=True), L2_EPS)
    k = k * (1.0 + (a - 1.0) * k_a)
    rh, kh, vh, ah = heads(r), heads(k), heads(v), heads(a)

    def step(S, inp):
        r_t, w_t, k_t, v_t, kk_t, a_t = inp
        sa = jnp.einsum('bhvk,bhk->bhv', S, -kk_t)
        S = (S * w_t[:, :, None, :]
             + sa[..., None] * (kk_t * a_t)[:, :, None, :]
             + v_t[..., None] * k_t[:, :, None, :])
        o_t = jnp.einsum('bhvk,bhk->bhv', S, r_t)
        return S, o_t

    xs = tuple(jnp.moveaxis(t, 1, 0) for t in (rh, heads(decay), kh, vh, kk, ah))
    S0 = jnp.zeros((B, H, N, N), jnp.float32)
    _, o = lax.scan(step, S0, xs)
    o = jnp.moveaxis(o, 0, 1)
    mu = jnp.mean(o, axis=-1, keepdims=True)
    var = jnp.mean(jnp.square(o - mu), axis=-1, keepdims=True)
    o = ((o - mu) * lax.rsqrt(var + GN_EPS)).reshape(B, T, RWKV_WIDTH) * lnx_g + lnx_b
    bonus = jnp.sum(rh * kh * r_k, axis=-1, keepdims=True) * vh
    y = (o + bonus.reshape(B, T, RWKV_WIDTH)) * g
    return y.astype(out_dtype)


def conformer_conv(u, conv_w, conv_b, ln_g, ln_b):
    a, b = jnp.split(u, 2, axis=-1)
    h = a * jax.nn.sigmoid(b)
    h = lax.conv_general_dilated(
        h, conv_w[:, None, :].astype(h.dtype), window_strides=(1,),
        padding=[(CONV_K - 1, 0)], dimension_numbers=('NWC', 'WIO', 'NWC'),
        feature_group_count=CONV_WIDTH) + conv_b
    hf = h.astype(jnp.float32)
    mu = jnp.mean(hf, axis=-1, keepdims=True)
    var = jnp.mean(jnp.square(hf - mu), axis=-1, keepdims=True)
    hf = (hf - mu) * lax.rsqrt(var + LN_EPS) * ln_g + ln_b
    return jax.nn.silu(hf).astype(u.dtype)


def setup_inputs(seed: int = 0) -> dict:
    key = jax.random.key(seed)
    ks = iter(jax.random.split(key, 32))
    nrm = lambda shape, s: jax.random.normal(next(ks), shape, jnp.float32) * s
    L, D = DEPTH, D_MODEL
    return {
        "x": nrm((BATCH, SEQ, D), 1.0),
        "c": nrm((BATCH, D), 1.0),
        "w_mod": nrm((L, D, 6 * D), 0.5 * D ** -0.5),
        "b_mod": nrm((L, 6 * D), 0.02),
        "norm1_g": 1.0 + nrm((L, D), 0.02),
        "w_in": nrm((L, D, N_IN_COLS), D ** -0.5),
        "mu_shift": jax.random.uniform(next(ks), (L, N_SHIFT_COLS), jnp.float32, 0.1, 0.9),
        "w0": jax.random.uniform(next(ks), (L, RWKV_WIDTH), jnp.float32, -6.0, 1.0),
        "w_decay_up": nrm((L, DECAY_LORA, RWKV_WIDTH), 0.5 * DECAY_LORA ** -0.5),
        "a0": nrm((L, RWKV_WIDTH), 0.3),
        "w_a_up": nrm((L, A_LORA, RWKV_WIDTH), 0.5 * A_LORA ** -0.5),
        "w_g_up": nrm((L, GATE_LORA, RWKV_WIDTH), GATE_LORA ** -0.5),
        "k_k": 0.85 + nrm((L, RWKV_WIDTH), 0.05),
        "k_a": 1.0 + nrm((L, RWKV_WIDTH), 0.05),
        "r_k": nrm((L, N_RWKV_HEADS, HEAD_DIM), 0.1),
        "lnx_g": 1.0 + nrm((L, RWKV_WIDTH), 0.02),
        "lnx_b": nrm((L, RWKV_WIDTH), 0.02),
        "conv_w": nrm((L, CONV_K, CONV_WIDTH), CONV_K ** -0.5),
        "conv_b": nrm((L, CONV_WIDTH), 0.02),
        "ln_conv_g": 1.0 + nrm((L, CONV_WIDTH), 0.02),
        "ln_conv_b": nrm((L, CONV_WIDTH), 0.02),
        "w_out": nrm((L, MIX_WIDTH, D), MIX_WIDTH ** -0.5),
        "norm2_g": 1.0 + nrm((L, D), 0.02),
        "w_ff1": nrm((L, D, D_FF), D ** -0.5),
        "w_ff2": nrm((L, D_FF, D), D_FF ** -0.5),
        "final_g": 1.0 + nrm((D,), 0.02),
    }


def reference(x, c, w_mod, b_mod, norm1_g, w_in, mu_shift, w0, w_decay_up, a0, w_a_up,
              w_g_up, k_k, k_a, r_k, lnx_g, lnx_b, conv_w, conv_b, ln_conv_g, ln_conv_b,
              w_out, norm2_g, w_ff1, w_ff2, final_g):
    for l in range(DEPTH):
        mod = (jax.nn.silu(c) @ w_mod[l] + b_mod[l])[:, None, :]
        sh1, sc1, g1, sh2, sc2, g2 = jnp.split(mod, 6, axis=-1)
        h = rmsnorm(x, norm1_g[l]) * (1.0 + sc1) + sh1
        p = h @ w_in[l]
        p_rwkv = token_shift(p[..., :N_SHIFT_COLS], mu_shift[l])
        y_a = rwkv7_time_mix(p_rwkv, w0[l], w_decay_up[l], a0[l], w_a_up[l], w_g_up[l],
                             k_k[l], k_a[l], r_k[l], lnx_g[l], lnx_b[l])
        y_b = conformer_conv(p[..., N_SHIFT_COLS:], conv_w[l], conv_b[l],
                             ln_conv_g[l], ln_conv_b[l])
        x = x + g1 * (jnp.concatenate([y_a, y_b], axis=-1) @ w_out[l])
        h = rmsnorm(x, norm2_g[l]) * (1.0 + sc2) + sh2
        x = x + g2 * (jnp.square(jax.nn.relu(h @ w_ff1[l])) @ w_ff2[l])
    return rmsnorm(x, final_g)
```

```python
import functools
import math

import jax
import jax.numpy as jnp
from jax import lax
from jax.experimental import pallas as pl
from jax.experimental.pallas import tpu as pltpu

F32 = jnp.float32
BF16 = jnp.bfloat16

HEAD_DIM = 64
HEAD_SHIFT = 6
CONV_K = 31
RMS_EPS = 1e-6
LN_EPS = 1e-5
GN_EPS = 64e-5
L2_EPS = 1e-12

LANES = 128
PAIR = 2 * HEAD_DIM
CHUNK = 64
VMEM_LIMIT = 56 * 1024 * 1024


def _tile(dim, pref, unit=LANES):
    t = min(dim, pref) // unit * unit
    while t > unit and dim % t:
        t -= unit
    assert t > 0 and dim % t == 0, (dim, pref)
    return t


def _bdot(a, b):
    return jnp.dot(a.astype(BF16), b.astype(BF16), preferred_element_type=F32)


def _split2(x):
    hi = x.astype(BF16)
    lo = (x - hi.astype(F32)).astype(BF16)
    return hi, lo


def _mod_kernel(c_ref, w_ref, b_ref, o_ref):
    c = c_ref[...]
    s = c * (1.0 / (1.0 + jnp.exp(-c)))
    s_hi, s_lo = _split2(s)
    w = w_ref[...]
    w_hi, w_lo = _split2(w)
    n = s.shape[0]
    top = jnp.dot(jnp.concatenate([s_hi, s_lo], axis=0), w_hi, preferred_element_type=F32)
    o_ref[...] = top[:n] + top[n:] + jnp.dot(s_hi, w_lo, preferred_element_type=F32) + b_ref[...]


def _mod(c, w_mod, b_mod):
    bsz, d = c.shape
    n = w_mod.shape[1]
    rows = 8
    c_pad = jnp.zeros((rows, d), F32).at[:bsz].set(c)
    tn = _tile(n, 512)
    out = pl.pallas_call(
        _mod_kernel,
        name="adaln_mod",
        grid=(n // tn,),
        in_specs=[pl.BlockSpec((rows, d), lambda j: (0, 0)),
                  pl.BlockSpec((d, tn), lambda j: (0, j)),
                  pl.BlockSpec((1, tn), lambda j: (0, j))],
        out_specs=pl.BlockSpec((rows, tn), lambda j: (0, j)),
        out_shape=jax.ShapeDtypeStruct((rows, n), F32),
        compiler_params=pltpu.CompilerParams(dimension_semantics=("parallel",),
                                             vmem_limit_bytes=VMEM_LIMIT),
    )(c_pad, w_mod, b_mod.reshape(1, n))
    return out[:bsz]


def _norm_mod_kernel(x_ref, g_ref, mod_ref, o_ref, *, shift_row, scale_row):
    x = x_ref[...]
    ms = jnp.mean(x * x, axis=-1, keepdims=True)
    y = x * lax.rsqrt(ms + RMS_EPS) * g_ref[...]
    sh = mod_ref[shift_row:shift_row + 1, :]
    sc = mod_ref[scale_row:scale_row + 1, :]
    o_ref[...] = (y * (1.0 + sc) + sh).astype(o_ref.dtype)


def _norm_mod(x2, g, mod3, seq, shift_row, scale_row):
    m, d = x2.shape
    tm = _tile(seq, 512)
    per_b = seq // tm
    return pl.pallas_call(
        functools.partial(_norm_mod_kernel, shift_row=shift_row, scale_row=scale_row),
        name="norm_mod",
        grid=(m // tm,),
        in_specs=[pl.BlockSpec((tm, d), lambda i: (i, 0)),
                  pl.BlockSpec((1, d), lambda i: (0, 0)),
                  pl.BlockSpec((None, 8, d), lambda i: (i // per_b, 0, 0))],
        out_specs=pl.BlockSpec((tm, d), lambda i: (i, 0)),
        out_shape=jax.ShapeDtypeStruct((m, d), BF16),
        compiler_params=pltpu.CompilerParams(dimension_semantics=("parallel",),
                                             vmem_limit_bytes=VMEM_LIMIT),
    )(x2, g.reshape(1, d), mod3)


def _matmul_kernel(a_ref, b_ref, o_ref):
    o_ref[...] = jnp.dot(a_ref[...], b_ref[...], preferred_element_type=F32)


def _matmul(a, b, tm_pref, tn_pref):
    m, k = a.shape
    n = b.shape[1]
    tm = _tile(m, tm_pref)
    tn = _tile(n, tn_pref)
    return pl.pallas_call(
        _matmul_kernel,
        name="in_proj",
        grid=(m // tm, n // tn),
        in_specs=[pl.BlockSpec((tm, k), lambda i, j: (i, 0)),
                  pl.BlockSpec((k, tn), lambda i, j: (0, j))],
        out_specs=pl.BlockSpec((tm, tn), lambda i, j: (i, j)),
        out_shape=jax.ShapeDtypeStruct((m, n), F32),
        compiler_params=pltpu.CompilerParams(dimension_semantics=("parallel", "parallel"),
                                             vmem_limit_bytes=VMEM_LIMIT),
    )(a, b)


def _shift_mix(x, halo_row, mu, is_first):
    prev = pltpu.roll(x, 1, axis=0)
    row = lax.broadcasted_iota(jnp.int32, x.shape, 0)
    first = jnp.where(is_first, jnp.zeros_like(halo_row), halo_row)
    prev = jnp.where(row == 0, first, prev)
    return x + (prev - x) * mu


def _segsum(x, ones_bd):
    hi, lo = _split2(x)
    n = x.shape[0]
    s = jnp.dot(jnp.concatenate([hi, lo], axis=0), ones_bd, preferred_element_type=F32)
    return s[:n] + s[n:]


def _stack_heads(x, lane_lo):
    return jnp.concatenate([jnp.where(lane_lo, x, 0.0), jnp.where(lane_lo, 0.0, x)], axis=0)


def _unit_lower_inverse(a, row, col):
    def same_block(shift):
        return (row >> shift) == (col >> shift)

    eye = jnp.where(row == col, 1.0, 0.0).astype(F32)
    blk = same_block(3)
    a8 = jnp.where(blk, a, 0.0)
    a2 = _bdot(a8, a8)
    a4 = _bdot(a2, a2)
    t = eye + a8
    t = t + _bdot(t, a2)
    t = t + _bdot(t, a4)
    for shift in (4, 5, 6):
        wider = same_block(shift)
        off = jnp.where(jnp.logical_and(wider, jnp.logical_not(blk)), a, 0.0)
        t = t + _bdot(_bdot(t, off), t)
        blk = wider
    return t


def _rwkv_kernel(pr_ref, pk_ref, pv_ref, pl_ref, hr_ref, hk_ref, hv_ref, hl_ref,
                 mus_ref, mul_ref, vec_ref, wd_ref, wa_ref, wg_ref, o_ref,
                 r_s, k_s, v_s, kk_s, b_s, lw_s, bonus_s, g_s, z_s):
    t_idx = pl.program_id(2)
    is_first = t_idx == 0
    tb, gl = pr_ref.shape
    n_pairs = gl // PAIR
    n_chunks = tb // CHUNK

    @pl.when(is_first)
    def _():
        z_s[...] = jnp.zeros_like(z_s)

    r = _shift_mix(pr_ref[...], hr_ref[7:8, :], mus_ref[0:1, :], is_first)
    k = _shift_mix(pk_ref[...], hk_ref[7:8, :], mus_ref[1:2, :], is_first)
    v = _shift_mix(pv_ref[...], hv_ref[7:8, :], mus_ref[2:3, :], is_first)
    lo = _shift_mix(pl_ref[...], hl_ref[7:8, :], mul_ref[...], is_first)

    w0 = vec_ref[0:1, :]
    a0 = vec_ref[1:2, :]
    k_k = vec_ref[2:3, :]
    k_a = vec_ref[3:4, :]
    r_k = vec_ref[4:5, :]

    lrow = lax.broadcasted_iota(jnp.int32, (gl, gl), 0)
    lcol = lax.broadcasted_iota(jnp.int32, (gl, gl), 1)
    ones_g = jnp.where((lrow >> HEAD_SHIFT) == (lcol >> HEAD_SHIFT), 1.0, 0.0).astype(BF16)

    pre_w = w0 + _bdot(jnp.tanh(lo), wd_ref[...])
    lw = (-math.exp(-0.5)) / (1.0 + jnp.exp(-pre_w))
    a = 1.0 / (1.0 + jnp.exp(-(a0 + _bdot(lo, wa_ref[...]))))
    g = _bdot(1.0 / (1.0 + jnp.exp(-lo)), wg_ref[...])
    kk = k * k_k
    nrm = jnp.sqrt(_segsum(kk * kk, ones_g))
    kk = kk / jnp.maximum(nrm, L2_EPS)
    km = k * (1.0 + (a - 1.0) * k_a)
    bonus = _segsum(r * km * r_k, ones_g) * v

    r_s[...] = r
    k_s[...] = km
    v_s[...] = v
    kk_s[...] = kk
    b_s[...] = kk * a
    lw_s[...] = lw
    bonus_s[...] = bonus
    g_s[...] = g

    crow = lax.broadcasted_iota(jnp.int32, (CHUNK, CHUNK), 0)
    ccol = lax.broadcasted_iota(jnp.int32, (CHUNK, CHUNK), 1)
    tri_c = jnp.where(ccol <= crow, 1.0, 0.0).astype(BF16)
    row = lax.broadcasted_iota(jnp.int32, (PAIR, PAIR), 0)
    col = lax.broadcasted_iota(jnp.int32, (PAIR, PAIR), 1)
    strict = col < row
    incl = col <= row
    diag = row == col
    ones_p = jnp.where((row >> HEAD_SHIFT) == (col >> HEAD_SHIFT), 1.0, 0.0).astype(BF16)
    lane_lo = lax.broadcasted_iota(jnp.int32, (CHUNK, PAIR), 1) < HEAD_DIM

    for p in range(n_pairs):
        ls = slice(p * PAIR, (p + 1) * PAIR)
        lnx_g = vec_ref[5:6, ls]
        lnx_b = vec_ref[6:7, ls]
        for c in range(n_chunks):
            rs = slice(c * CHUNK, (c + 1) * CHUNK)
            rc = r_s[rs, ls]
            kc = k_s[rs, ls]
            vc = v_s[rs, ls]
            kkc = kk_s[rs, ls]
            bc = b_s[rs, ls]
            lwc = lw_s[rs, ls]

            hi = lwc.astype(BF16)
            r1 = lwc - hi.astype(F32)
            mid = r1.astype(BF16)
            lo3 = (r1 - mid.astype(F32)).astype(BF16)
            cs3 = jnp.dot(tri_c, jnp.concatenate([hi, mid, lo3], axis=1), preferred_element_type=F32)
            cs = cs3[:, :PAIR] + cs3[:, PAIR:2 * PAIR] + cs3[:, 2 * PAIR:]
            cs_end = cs[CHUNK - 1:CHUNK, :]

            e_pos = jnp.exp(cs)
            e_neg = jnp.exp(-cs)
            e_end = jnp.exp(cs_end - cs)
            a_t = -kkc * jnp.exp(cs - lwc)
            r_t = rc * e_pos
            b_t = bc * e_neg
            k_t = kc * e_neg
            b_h = bc * e_end
            k_h = kc * e_end
            g_end = jnp.exp(cs_end)

            a_st = _stack_heads(a_t, lane_lo)
            r_st = _stack_heads(r_t, lane_lo)
            v_st = _stack_heads(vc, lane_lo).astype(BF16)

            lhs = jnp.concatenate([a_st, r_st], axis=0).astype(BF16)
            rhs = jnp.concatenate([_stack_heads(b_t, lane_lo), _stack_heads(k_t, lane_lo)], axis=0).astype(BF16)
            aa = lax.dot_general(lhs, rhs, (((1,), (1,)), ((), ())), preferred_element_type=F32)
            a_ab = jnp.where(strict, aa[:PAIR, :PAIR], 0.0)
            a_ak = jnp.where(strict, aa[:PAIR, PAIR:], 0.0)
            a_rb = jnp.where(incl, aa[PAIR:, :PAIR], 0.0)
            a_rk = jnp.where(incl, aa[PAIR:, PAIR:], 0.0)

            t_inv = _unit_lower_inverse(a_ab, row, col)

            akv = jnp.dot(a_ak.astype(BF16), v_st, preferred_element_type=F32)
            tx = _bdot(t_inv, jnp.concatenate([a_st, akv], axis=1)).astype(BF16)
            pq = jnp.dot(a_rb.astype(BF16), tx, preferred_element_type=F32)
            p_st = r_st + pq[:, :PAIR]
            q_st = pq[:, PAIR:] + jnp.dot(a_rk.astype(BF16), v_st, preferred_element_type=F32)

            bh_t = _stack_heads(b_h, lane_lo).T.astype(BF16)
            kh_t = _stack_heads(k_h, lane_lo).T.astype(BF16)
            mn = jnp.dot(bh_t, tx, preferred_element_type=F32)
            m_mat = jnp.where(diag, g_end, 0.0) + mn[:, :PAIR]
            n_mat = mn[:, PAIR:] + jnp.dot(kh_t, v_st, preferred_element_type=F32)

            z = z_s[p]
            o_st = _bdot(p_st, z) + q_st
            z_s[p] = _bdot(m_mat, z) + n_mat
            o = o_st[:CHUNK] + o_st[CHUNK:]

            mean = _segsum(o, ones_p) * (1.0 / HEAD_DIM)
            d = o - mean
            var = _segsum(d * d, ones_p) * (1.0 / HEAD_DIM)
            y = d * lax.rsqrt(var + GN_EPS) * lnx_g + lnx_b
            y = (y + bonus_s[rs, ls]) * g_s[rs, ls]
            o_ref[rs, ls] = y.astype(o_ref.dtype)


def _rwkv(p, seq, rw, lp, col0, mus, mul, vecs, wd, wa, wg):
    m = p.shape[0]
    bsz = m // seq
    gl = _tile(rw, 256)
    tb = _tile(seq, 256)
    n_t = seq // tb
    n_g = rw // gl
    halo = 8

    def blk(col0):
        return pl.BlockSpec((tb, gl), lambda b, g, t: (b * n_t + t, col0 + g))

    def halo_blk(col0):
        return pl.BlockSpec(
            (halo, gl), lambda b, g, t: (jnp.maximum((b * n_t + t) * (tb // halo) - 1, 0), col0 + g))

    assert col0 % gl == 0 and (col0 + 3 * rw) % lp == 0
    c_r = col0 // gl
    c_k = c_r + rw // gl
    c_v = c_k + rw // gl
    lora_col = (col0 + 3 * rw) // lp
    in_specs = [
        blk(c_r), blk(c_k), blk(c_v),
        pl.BlockSpec((tb, lp), lambda b, g, t: (b * n_t + t, lora_col)),
        halo_blk(c_r), halo_blk(c_k), halo_blk(c_v),
        pl.BlockSpec((halo, lp), lambda b, g, t: (jnp.maximum((b * n_t + t) * (tb // halo) - 1, 0), lora_col)),
        pl.BlockSpec((8, gl), lambda b, g, t: (0, g)),
        pl.BlockSpec((1, lp), lambda b, g, t: (0, 0)),
        pl.BlockSpec((8, gl), lambda b, g, t: (0, g)),
        pl.BlockSpec((lp, gl), lambda b, g, t: (0, g)),
        pl.BlockSpec((lp, gl), lambda b, g, t: (0, g)),
        pl.BlockSpec((lp, gl), lambda b, g, t: (0, g)),
    ]
    scratch = [pltpu.VMEM((tb, gl), F32) for _ in range(8)]
    scratch.append(pltpu.VMEM((gl // PAIR, PAIR, PAIR), F32))
    return pl.pallas_call(
        _rwkv_kernel,
        name="rwkv7_mix",
        grid=(bsz, n_g, n_t),
        in_specs=in_specs,
        out_specs=pl.BlockSpec((tb, gl), lambda b, g, t: (b * n_t + t, g)),
        out_shape=jax.ShapeDtypeStruct((m, rw), BF16),
        scratch_shapes=scratch,
        compiler_params=pltpu.CompilerParams(
            dimension_semantics=("parallel", "parallel", "arbitrary"), vmem_limit_bytes=VMEM_LIMIT),
    )(p, p, p, p, p, p, p, p, mus, mul, vecs, wd, wa, wg)


def _conv_kernel(a_ref, b_ref, ha_ref, hb_ref, w_ref, vec_ref, o_ref, h_s, *, halo):
    is_first = pl.program_id(1) == 0
    tm = a_ref.shape[0]

    def glu(a, b):
        return a * (1.0 / (1.0 + jnp.exp(-b)))

    h_prev = glu(ha_ref[...], hb_ref[...])
    h_s[0:halo, :] = jnp.where(is_first, jnp.zeros_like(h_prev), h_prev)
    h_s[halo:halo + tm, :] = glu(a_ref[...], b_ref[...])

    acc = jnp.zeros(a_ref.shape, F32) + vec_ref[0:1, :]
    for j in range(CONV_K):
        off = halo - (CONV_K - 1) + j
        acc = acc + h_s[off:off + tm, :] * w_ref[j:j + 1, :]
    mu = jnp.mean(acc, axis=-1, keepdims=True)
    d = acc - mu
    var = jnp.mean(d * d, axis=-1, keepdims=True)
    y = d * lax.rsqrt(var + LN_EPS) * vec_ref[1:2, :] + vec_ref[2:3, :]
    o_ref[...] = (y * (1.0 / (1.0 + jnp.exp(-y)))).astype(o_ref.dtype)


def _conv(p, seq, cw, col0, conv_w_pad, vecs):
    m = p.shape[0]
    bsz = m // seq
    tm = _tile(seq, 256)
    n_t = seq // tm
    halo = 32
    ca = col0 // cw
    cb = ca + 1

    def blk(cblk):
        return pl.BlockSpec((tm, cw), lambda b, t: (b * n_t + t, cblk))

    def halo_blk(cblk):
        return pl.BlockSpec(
            (halo, cw), lambda b, t: (jnp.maximum((b * n_t + t) * (tm // halo) - 1, 0), cblk))

    return pl.pallas_call(
        functools.partial(_conv_kernel, halo=halo),
        name="conformer_conv",
        grid=(bsz, n_t),
        in_specs=[blk(ca), blk(cb), halo_blk(ca), halo_blk(cb),
                  pl.BlockSpec((32, cw), lambda b, t: (0, 0)),
                  pl.BlockSpec((8, cw), lambda b, t: (0, 0))],
        out_specs=pl.BlockSpec((tm, cw), lambda b, t: (b * n_t + t, 0)),
        out_shape=jax.ShapeDtypeStruct((m, cw), BF16),
        scratch_shapes=[pltpu.VMEM((halo + tm, cw), F32)],
        compiler_params=pltpu.CompilerParams(
            dimension_semantics=("parallel", "arbitrary"), vmem_limit_bytes=VMEM_LIMIT),
    )(p, p, p, p, conv_w_pad, vecs)


def _out_kernel(ya_ref, yb_ref, wa_ref, wb_ref, x_ref, mod_ref, o_ref, *, gate_row):
    acc = jnp.dot(ya_ref[...], wa_ref[...], preferred_element_type=F32)
    acc = acc + jnp.dot(yb_ref[...], wb_ref[...], preferred_element_type=F32)
    o_ref[...] = x_ref[...] + mod_ref[gate_row:gate_row + 1, :] * acc


def _out_proj(ya, yb, w_out_bf, x2, mod3, seq, gate_row):
    m, rw = ya.shape
    cw = yb.shape[1]
    d = x2.shape[1]
    assert rw == cw
    tm = _tile(seq, 1024)
    tn = _tile(d, 512)
    per_b = seq // tm
    return pl.pallas_call(
        functools.partial(_out_kernel, gate_row=gate_row),
        name="out_proj",
        grid=(m // tm, d // tn),
        in_specs=[pl.BlockSpec((tm, rw), lambda i, j: (i, 0)),
                  pl.BlockSpec((tm, cw), lambda i, j: (i, 0)),
                  pl.BlockSpec((rw, tn), lambda i, j: (0, j)),
                  pl.BlockSpec((cw, tn), lambda i, j: (1, j)),
                  pl.BlockSpec((tm, tn), lambda i, j: (i, j)),
                  pl.BlockSpec((None, 8, tn), lambda i, j: (i // per_b, 0, j))],
        out_specs=pl.BlockSpec((tm, tn), lambda i, j: (i, j)),
        out_shape=jax.ShapeDtypeStruct((m, d), F32),
        compiler_params=pltpu.CompilerParams(dimension_semantics=("parallel", "parallel"),
                                             vmem_limit_bytes=VMEM_LIMIT),
    )(ya, yb, w_out_bf, w_out_bf, x2, mod3)


def _ffn_kernel(h_ref, w1_ref, w2_ref, x_ref, mod_ref, fg_ref, o_ref, *, gate_row, final_norm):
    j = pl.program_id(1)
    hid = jnp.dot(h_ref[...], w1_ref[...], preferred_element_type=F32)
    hid = jnp.maximum(hid, 0.0)
    hid = (hid * hid).astype(BF16)

    @pl.when(j == 0)
    def _():
        o_ref[...] = jnp.zeros_like(o_ref)

    d = o_ref.shape[1]
    tn = _tile(d, 1024)
    for n in range(d // tn):
        cols = slice(n * tn, (n + 1) * tn)
        o_ref[:, cols] += jnp.dot(hid, w2_ref[:, cols], preferred_element_type=F32)

    @pl.when(j == pl.num_programs(1) - 1)
    def _():
        y = x_ref[...] + mod_ref[gate_row:gate_row + 1, :] * o_ref[...]
        if final_norm:
            ms = jnp.mean(y * y, axis=-1, keepdims=True)
            y = y * lax.rsqrt(ms + RMS_EPS) * fg_ref[...]
        o_ref[...] = y


def _ffn(h, w1_bf, w2_bf, x1, mod3, final_g, seq, gate_row, final_norm):
    m, d = h.shape
    dff = w1_bf.shape[1]
    tm = _tile(seq, 512)
    tf = _tile(dff, 512)
    per_b = seq // tm
    return pl.pallas_call(
        functools.partial(_ffn_kernel, gate_row=gate_row, final_norm=final_norm),
        name="ffn",
        grid=(m // tm, dff // tf),
        in_specs=[pl.BlockSpec((tm, d), lambda i, j: (i, 0), pipeline_mode=pl.Buffered(1)),
                  pl.BlockSpec((d, tf), lambda i, j: (0, j)),
                  pl.BlockSpec((tf, d), lambda i, j: (j, 0)),
                  pl.BlockSpec((tm, d), lambda i, j: (i, 0), pipeline_mode=pl.Buffered(1)),
                  pl.BlockSpec((None, 8, d), lambda i, j: (i // per_b, 0, 0)),
                  pl.BlockSpec((1, d), lambda i, j: (0, 0))],
        out_specs=pl.BlockSpec((tm, d), lambda i, j: (i, 0)),
        out_shape=jax.ShapeDtypeStruct((m, d), F32),
        compiler_params=pltpu.CompilerParams(dimension_semantics=("parallel", "arbitrary"),
                                             vmem_limit_bytes=VMEM_LIMIT),
    )(h, w1_bf, w2_bf, x1, mod3, final_g.reshape(1, d))


def _rows8(rows, width):
    out = jnp.zeros((8, width), F32)
    for i, r in enumerate(rows):
        out = out.at[i].set(r.reshape(width).astype(F32))
    return out


def kernel(x, c, w_mod, b_mod, norm1_g, w_in, mu_shift, w0, w_decay_up, a0, w_a_up, w_g_up, k_k, k_a, r_k,
           lnx_g, lnx_b, conv_w, conv_b, ln_conv_g, ln_conv_b, w_out, norm2_g, w_ff1, w_ff2, final_g):
    bsz, seq, d = x.shape
    depth = w_mod.shape[0]
    rw = w0.shape[1]
    cw = conv_b.shape[1]
    n_dec, n_a, n_gate = w_decay_up.shape[1], w_a_up.shape[1], w_g_up.shape[1]
    n_lora = n_dec + n_a + n_gate
    lp = -(-n_lora // LANES) * LANES
    n_shift = 3 * rw + n_lora
    m = bsz * seq
    x2 = x.reshape(m, d)

    for l in range(depth):
        mod = _mod(c, w_mod[l], b_mod[l])
        mod3 = jnp.concatenate([mod.reshape(bsz, 6, d), jnp.zeros((bsz, 2, d), F32)], axis=1)

        w_in_p = jnp.concatenate(
            [w_in[l][:, n_shift:], w_in[l][:, :n_shift], jnp.zeros((d, lp - n_lora), F32)], axis=1).astype(BF16)
        mu = mu_shift[l]
        mus = _rows8([mu[:rw], mu[rw:2 * rw], mu[2 * rw:3 * rw]], rw)
        mul = jnp.zeros((1, lp), F32).at[0, :n_lora].set(mu[3 * rw:])
        vecs = _rows8([w0[l], a0[l], k_k[l], k_a[l], r_k[l], lnx_g[l], lnx_b[l]], rw)
        wd = jnp.zeros((lp, rw), F32).at[:n_dec].set(w_decay_up[l]).astype(BF16)
        wa = jnp.zeros((lp, rw), F32).at[n_dec:n_dec + n_a].set(w_a_up[l]).astype(BF16)
        wg = jnp.zeros((lp, rw), F32).at[n_dec + n_a:n_lora].set(w_g_up[l]).astype(BF16)
        conv_w_pad = jnp.zeros((32, cw), F32).at[:CONV_K].set(conv_w[l])
        conv_vecs = _rows8([conv_b[l], ln_conv_g[l], ln_conv_b[l]], cw)

        h1 = _norm_mod(x2, norm1_g[l], mod3, seq, shift_row=0, scale_row=1)
        p = _matmul(h1, w_in_p, 1024, 768)
        y_a = _rwkv(p, seq, rw, lp, 2 * cw, mus, mul, vecs, wd, wa, wg)
        y_b = _conv(p, seq, cw, 0, conv_w_pad, conv_vecs)
        x1 = _out_proj(y_a, y_b, w_out[l].astype(BF16), x2, mod3, seq, gate_row=2)
        h2 = _norm_mod(x1, norm2_g[l], mod3, seq, shift_row=3, scale_row=4)
        x2 = _ffn(h2, w_ff1[l].astype(BF16), w_ff2[l].astype(BF16), x1, mod3, final_g, seq, gate_row=5,
                  final_norm=(l == depth - 1))
    return x2.reshape(bsz, seq, d)
```

```python
import functools
import math

import jax
import jax.numpy as jnp
from jax import lax
from jax.experimental import pallas as pl
from jax.experimental.pallas import tpu as pltpu

F32 = jnp.float32
BF16 = jnp.bfloat16

HEAD_DIM = 64
HEAD_SHIFT = 6
CONV_K = 31
RMS_EPS = 1e-6
LN_EPS = 1e-5
GN_EPS = 64e-5
L2_EPS = 1e-12

LANES = 128
SUBLANES = 8
CONV_ROWS = 128
PAIR = 2 * HEAD_DIM
CHUNK_SHIFT = 6
CHUNK = 64
VMEM_LIMIT = 56 * 1024 * 1024


def _tile(dim, pref, unit=LANES):
    t = min(dim, pref) // unit * unit
    while t > unit and dim % t:
        t -= unit
    assert t > 0 and dim % t == 0, (dim, pref)
    return t


def _bdot(a, b):
    return jnp.dot(a.astype(BF16), b.astype(BF16), preferred_element_type=F32)


def _split2(x):
    hi = x.astype(BF16)
    lo = (x - hi.astype(F32)).astype(BF16)
    return hi, lo


def _mod_kernel(c_ref, w_ref, b_ref, o_ref):
    c = c_ref[...]
    s = c * (1.0 / (1.0 + jnp.exp(-c)))
    s_hi, s_lo = _split2(s)
    w = w_ref[...]
    w_hi, w_lo = _split2(w)
    n = s.shape[0]
    top = jnp.dot(jnp.concatenate([s_hi, s_lo], axis=0), w_hi, preferred_element_type=F32)
    o_ref[...] = top[:n] + top[n:] + jnp.dot(s_hi, w_lo, preferred_element_type=F32) + b_ref[...]


def _mod(c, w_mod, b_mod):
    bsz, d = c.shape
    n = w_mod.shape[1]
    rows = 8
    c_pad = jnp.zeros((rows, d), F32).at[:bsz].set(c)
    tn = _tile(n, 512)
    out = pl.pallas_call(
        _mod_kernel,
        name="adaln_mod",
        grid=(n // tn,),
        in_specs=[pl.BlockSpec((rows, d), lambda j: (0, 0)),
                  pl.BlockSpec((d, tn), lambda j: (0, j)),
                  pl.BlockSpec((1, tn), lambda j: (0, j))],
        out_specs=pl.BlockSpec((rows, tn), lambda j: (0, j)),
        out_shape=jax.ShapeDtypeStruct((rows, n), F32),
        compiler_params=pltpu.CompilerParams(dimension_semantics=("parallel",),
                                             vmem_limit_bytes=VMEM_LIMIT),
    )(c_pad, w_mod, b_mod.reshape(1, n))
    return out[:bsz]


def _norm_mod_kernel(x_ref, g_ref, mod_ref, o_ref, *, shift_row, scale_row):
    x = x_ref[...]
    ms = jnp.mean(x * x, axis=-1, keepdims=True)
    y = x * lax.rsqrt(ms + RMS_EPS) * g_ref[...]
    sh = mod_ref[shift_row:shift_row + 1, :]
    sc = mod_ref[scale_row:scale_row + 1, :]
    o_ref[...] = (y * (1.0 + sc) + sh).astype(o_ref.dtype)


def _norm_mod(x2, g, mod3, seq, shift_row, scale_row):
    m, d = x2.shape
    tm = _tile(seq, 512)
    per_b = seq // tm
    return pl.pallas_call(
        functools.partial(_norm_mod_kernel, shift_row=shift_row, scale_row=scale_row),
        name="norm_mod",
        grid=(m // tm,),
        in_specs=[pl.BlockSpec((tm, d), lambda i: (i, 0)),
                  pl.BlockSpec((1, d), lambda i: (0, 0)),
                  pl.BlockSpec((None, 8, d), lambda i: (i // per_b, 0, 0))],
        out_specs=pl.BlockSpec((tm, d), lambda i: (i, 0)),
        out_shape=jax.ShapeDtypeStruct((m, d), BF16),
        compiler_params=pltpu.CompilerParams(dimension_semantics=("parallel",),
                                             vmem_limit_bytes=VMEM_LIMIT),
    )(x2, g.reshape(1, d), mod3)


def _matmul_kernel(a_ref, b_ref, o_ref):
    o_ref[...] = jnp.dot(a_ref[...], b_ref[...], preferred_element_type=F32)


def _matmul(a, b, tm_pref, tn_pref):
    m, k = a.shape
    n = b.shape[1]
    tm = _tile(m, tm_pref)
    tn = _tile(n, tn_pref)
    return pl.pallas_call(
        _matmul_kernel,
        name="in_proj",
        grid=(m // tm, n // tn),
        in_specs=[pl.BlockSpec((tm, k), lambda i, j: (i, 0)),
                  pl.BlockSpec((k, tn), lambda i, j: (0, j))],
        out_specs=pl.BlockSpec((tm, tn), lambda i, j: (i, j)),
        out_shape=jax.ShapeDtypeStruct((m, n), F32),
        compiler_params=pltpu.CompilerParams(dimension_semantics=("parallel", "parallel"),
                                             vmem_limit_bytes=VMEM_LIMIT),
    )(a, b)


def _shift_mix(x, halo_row, mu, is_first):
    prev = pltpu.roll(x, 1, axis=0)
    row = lax.broadcasted_iota(jnp.int32, x.shape, 0)
    first = jnp.where(is_first, jnp.zeros_like(halo_row), halo_row)
    prev = jnp.where(row == 0, first, prev)
    return x + (prev - x) * mu


def _segsum(x, ones_bd):
    hi, lo = _split2(x)
    n = x.shape[0]
    s = jnp.dot(jnp.concatenate([hi, lo], axis=0), ones_bd, preferred_element_type=F32)
    return s[:n] + s[n:]


def _bmm(a, b):
    return jnp.einsum("bmk,bkn->bmn", a.astype(BF16), b.astype(BF16), preferred_element_type=F32)


def _unit_lower_inverse(a, row, col):
    def same_block(shift):
        return (row >> shift) == (col >> shift)

    eye = jnp.where(row == col, 1.0, 0.0).astype(F32)
    blk = same_block(3)
    a8 = jnp.where(blk, a, 0.0)
    a2 = _bmm(a8, a8)
    a4 = _bmm(a2, a2)
    t = eye + a8
    t = t + _bmm(t, a2)
    t = t + _bmm(t, a4)
    for shift in (4, 5, 6):
        wider = same_block(shift)
        off = jnp.where(jnp.logical_and(wider, jnp.logical_not(blk)), a, 0.0)
        t = t + _bmm(_bmm(t, off), t)
        blk = wider
    return t


def _rwkv_kernel(pr_ref, pk_ref, pv_ref, pl_ref, hr_ref, hk_ref, hv_ref, hl_ref,
                 mus_ref, mul_ref, vec_ref, wd_ref, wa_ref, wg_ref, o_ref, o_s, z_s):
    t_idx = pl.program_id(2)
    is_first = t_idx == 0
    tb, gl = pr_ref.shape
    n_pairs = gl // PAIR
    n_chunks = tb // CHUNK

    @pl.when(is_first)
    def _():
        z_s[...] = jnp.zeros_like(z_s)

    r = _shift_mix(pr_ref[...], hr_ref[7:8, :], mus_ref[0:1, :], is_first)
    k = _shift_mix(pk_ref[...], hk_ref[7:8, :], mus_ref[1:2, :], is_first)
    v = _shift_mix(pv_ref[...], hv_ref[7:8, :], mus_ref[2:3, :], is_first)
    lo = _shift_mix(pl_ref[...], hl_ref[7:8, :], mul_ref[...], is_first)

    w0 = vec_ref[0:1, :]
    a0 = vec_ref[1:2, :]
    k_k = vec_ref[2:3, :]
    k_a = vec_ref[3:4, :]
    r_k = vec_ref[4:5, :]

    lrow = lax.broadcasted_iota(jnp.int32, (gl, gl), 0)
    lcol = lax.broadcasted_iota(jnp.int32, (gl, gl), 1)
    ones_g = jnp.where((lrow >> HEAD_SHIFT) == (lcol >> HEAD_SHIFT), 1.0, 0.0).astype(BF16)

    pre_w = w0 + _bdot(jnp.tanh(lo), wd_ref[...])
    lw = (-math.exp(-0.5)) / (1.0 + jnp.exp(-pre_w))
    a = 1.0 / (1.0 + jnp.exp(-(a0 + _bdot(lo, wa_ref[...]))))
    g = _bdot(1.0 / (1.0 + jnp.exp(-lo)), wg_ref[...])
    kk = k * k_k
    nrm = jnp.sqrt(_segsum(kk * kk, ones_g))
    kk = kk / jnp.maximum(nrm, L2_EPS)
    km = k * (1.0 + (a - 1.0) * k_a)
    bonus = _segsum(r * km * r_k, ones_g) * v

    b = kk * a

    trow = lax.broadcasted_iota(jnp.int32, (tb, tb), 0)
    tcol = lax.broadcasted_iota(jnp.int32, (tb, tb), 1)
    tri = jnp.logical_and((trow >> CHUNK_SHIFT) == (tcol >> CHUNK_SHIFT), tcol <= trow)
    tri = jnp.where(tri, 1.0, 0.0).astype(BF16)
    hi = lw.astype(BF16)
    r1 = lw - hi.astype(F32)
    mid = r1.astype(BF16)
    lo3 = (r1 - mid.astype(F32)).astype(BF16)
    cs3 = jnp.dot(tri, jnp.concatenate([hi, mid, lo3], axis=1), preferred_element_type=F32)
    cs = cs3[:, :gl] + cs3[:, gl:2 * gl] + cs3[:, 2 * gl:]

    def chunks(x):
        return x.reshape(n_chunks, CHUNK, gl)

    cs_c = chunks(cs)
    cs_end = cs_c[:, CHUNK - 1:CHUNK, :]
    e_pos = jnp.exp(cs_c)
    e_neg = jnp.exp(-cs_c)
    e_end = jnp.exp(cs_end - cs_c)
    a_t = -chunks(kk) * jnp.exp(cs_c - chunks(lw))
    r_t = chunks(r) * e_pos
    b_t = chunks(b) * e_neg
    k_t = chunks(km) * e_neg
    b_h = chunks(b) * e_end
    k_h = chunks(km) * e_end
    g_end = jnp.exp(cs_end)

    lane_lo = lax.broadcasted_iota(jnp.int32, (1, 1, PAIR), 2) < HEAD_DIM

    def pairs(x):
        return jnp.concatenate([x[:, :, p * PAIR:(p + 1) * PAIR] for p in range(n_pairs)], axis=0)

    def stack(x):
        xp = pairs(x)
        return jnp.concatenate([jnp.where(lane_lo, xp, 0.0), jnp.where(lane_lo, 0.0, xp)], axis=1)

    row = lax.broadcasted_iota(jnp.int32, (PAIR, PAIR), 0)
    col = lax.broadcasted_iota(jnp.int32, (PAIR, PAIR), 1)
    strict = col < row
    incl = col <= row
    diag = row == col

    a_st = stack(a_t)
    r_st = stack(r_t)
    v_st = stack(chunks(v)).astype(BF16)
    lhs = jnp.concatenate([a_st, r_st], axis=1).astype(BF16)
    rhs = jnp.concatenate([stack(b_t), stack(k_t)], axis=1).astype(BF16)
    aa = jnp.einsum("bmk,bnk->bmn", lhs, rhs, preferred_element_type=F32)
    a_ab = jnp.where(strict, aa[:, :PAIR, :PAIR], 0.0)
    a_ak = jnp.where(strict, aa[:, :PAIR, PAIR:], 0.0)
    a_rb = jnp.where(incl, aa[:, PAIR:, :PAIR], 0.0)
    a_rk = jnp.where(incl, aa[:, PAIR:, PAIR:], 0.0)

    t_inv = _unit_lower_inverse(a_ab, row, col)

    akv = _bmm(a_ak, v_st)
    tx = _bmm(t_inv, jnp.concatenate([a_st, akv], axis=2)).astype(BF16)
    pq = _bmm(a_rb, tx)
    p_st = (r_st + pq[:, :, :PAIR]).astype(BF16)
    q_st = pq[:, :, PAIR:] + _bmm(a_rk, v_st)
    bh_t = jnp.swapaxes(stack(b_h), 1, 2)
    kh_t = jnp.swapaxes(stack(k_h), 1, 2)
    mn = _bmm(bh_t, tx)
    m_mat = (jnp.where(diag, pairs(g_end), 0.0) + mn[:, :, :PAIR]).astype(BF16)
    n_mat = mn[:, :, PAIR:] + _bmm(kh_t, v_st)

    zs = [z_s[p] for p in range(n_pairs)]
    for c in range(n_chunks):
        for p in range(n_pairs):
            i = p * n_chunks + c
            zb = zs[p].astype(BF16)
            o_st = jnp.dot(p_st[i], zb, preferred_element_type=F32) + q_st[i]
            zs[p] = jnp.dot(m_mat[i], zb, preferred_element_type=F32) + n_mat[i]
            o_s[c * CHUNK:(c + 1) * CHUNK, p * PAIR:(p + 1) * PAIR] = o_st[:CHUNK] + o_st[CHUNK:]
    for p in range(n_pairs):
        z_s[p] = zs[p]

    o = o_s[...]
    mean = _segsum(o, ones_g) * (1.0 / HEAD_DIM)
    d = o - mean
    var = _segsum(d * d, ones_g) * (1.0 / HEAD_DIM)
    y = d * lax.rsqrt(var + GN_EPS) * vec_ref[5:6, :] + vec_ref[6:7, :]
    o_ref[...] = ((y + bonus) * g).astype(o_ref.dtype)


def _rwkv(p, seq, rw, lp, col0, mus, mul, vecs, wd, wa, wg):
    m = p.shape[0]
    bsz = m // seq
    gl = _tile(rw, 256)
    tb = _tile(seq, 512)
    n_t = seq // tb
    n_g = rw // gl
    halo = 8

    def blk(col0):
        return pl.BlockSpec((tb, gl), lambda b, g, t: (b * n_t + t, col0 + g))

    def halo_blk(col0):
        return pl.BlockSpec(
            (halo, gl), lambda b, g, t: (jnp.maximum((b * n_t + t) * (tb // halo) - 1, 0), col0 + g))

    assert col0 % gl == 0 and (col0 + 3 * rw) % lp == 0
    c_r = col0 // gl
    c_k = c_r + rw // gl
    c_v = c_k + rw // gl
    lora_col = (col0 + 3 * rw) // lp
    in_specs = [
        blk(c_r), blk(c_k), blk(c_v),
        pl.BlockSpec((tb, lp), lambda b, g, t: (b * n_t + t, lora_col)),
        halo_blk(c_r), halo_blk(c_k), halo_blk(c_v),
        pl.BlockSpec((halo, lp), lambda b, g, t: (jnp.maximum((b * n_t + t) * (tb // halo) - 1, 0), lora_col)),
        pl.BlockSpec((8, gl), lambda b, g, t: (0, g)),
        pl.BlockSpec((1, lp), lambda b, g, t: (0, 0)),
        pl.BlockSpec((8, gl), lambda b, g, t: (0, g)),
        pl.BlockSpec((lp, gl), lambda b, g, t: (0, g)),
        pl.BlockSpec((lp, gl), lambda b, g, t: (0, g)),
        pl.BlockSpec((lp, gl), lambda b, g, t: (0, g)),
    ]
    scratch = [pltpu.VMEM((tb, gl), F32), pltpu.VMEM((gl // PAIR, PAIR, PAIR), F32)]
    return pl.pallas_call(
        _rwkv_kernel,
        name="rwkv7_mix",
        grid=(bsz, n_g, n_t),
        in_specs=in_specs,
        out_specs=pl.BlockSpec((tb, gl), lambda b, g, t: (b * n_t + t, g)),
        out_shape=jax.ShapeDtypeStruct((m, rw), BF16),
        scratch_shapes=scratch,
        compiler_params=pltpu.CompilerParams(
            dimension_semantics=("parallel", "parallel", "arbitrary"), vmem_limit_bytes=VMEM_LIMIT),
    )(p, p, p, p, p, p, p, p, mus, mul, vecs, wd, wa, wg)


def _conv_kernel(a_ref, b_ref, ha_ref, hb_ref, w_ref, vec_ref, o_ref, h_s, c_s, *, halo):
    is_first = pl.program_id(1) == 0
    tm, cw = a_ref.shape
    rb = _tile(tm, CONV_ROWS, SUBLANES)

    def glu(a, b):
        return a * (1.0 / (1.0 + jnp.exp(-b)))

    h_prev = glu(ha_ref[...], hb_ref[...])
    h_s[0:halo, :] = jnp.where(is_first, jnp.zeros_like(h_prev), h_prev)
    h_s[halo:halo + tm, :] = glu(a_ref[...], b_ref[...])
    h_s[halo + tm:halo + tm + SUBLANES, :] = jnp.zeros((SUBLANES, cw), F32)

    first = halo - (CONV_K - 1)
    for s in range(cw // LANES):
        ls = slice(s * LANES, (s + 1) * LANES)
        for r0 in range(0, tm, rb):
            acc = None
            for q in range(SUBLANES):
                part = None
                for j in range(CONV_K):
                    if (first + j) % SUBLANES != q:
                        continue
                    base = r0 + first + j - q
                    term = h_s[base:base + rb + SUBLANES, ls] * w_ref[j:j + 1, ls]
                    part = term if part is None else part + term
                if part is None:
                    continue
                shifted = part[q:q + rb, :]
                acc = shifted if acc is None else acc + shifted
            c_s[r0:r0 + rb, ls] = acc + vec_ref[0:1, ls]

    acc = c_s[...]
    mu = jnp.mean(acc, axis=-1, keepdims=True)
    d = acc - mu
    var = jnp.mean(d * d, axis=-1, keepdims=True)
    y = d * lax.rsqrt(var + LN_EPS) * vec_ref[1:2, :] + vec_ref[2:3, :]
    o_ref[...] = (y * (1.0 / (1.0 + jnp.exp(-y)))).astype(o_ref.dtype)


def _conv(p, seq, cw, col0, conv_w_pad, vecs):
    m = p.shape[0]
    bsz = m // seq
    tm = _tile(seq, 256)
    n_t = seq // tm
    halo = 32
    ca = col0 // cw
    cb = ca + 1

    def blk(cblk):
        return pl.BlockSpec((tm, cw), lambda b, t: (b * n_t + t, cblk))

    def halo_blk(cblk):
        return pl.BlockSpec(
            (halo, cw), lambda b, t: (jnp.maximum((b * n_t + t) * (tm // halo) - 1, 0), cblk))

    return pl.pallas_call(
        functools.partial(_conv_kernel, halo=halo),
        name="conformer_conv",
        grid=(bsz, n_t),
        in_specs=[blk(ca), blk(cb), halo_blk(ca), halo_blk(cb),
                  pl.BlockSpec((32, cw), lambda b, t: (0, 0)),
                  pl.BlockSpec((8, cw), lambda b, t: (0, 0))],
        out_specs=pl.BlockSpec((tm, cw), lambda b, t: (b * n_t + t, 0)),
        out_shape=jax.ShapeDtypeStruct((m, cw), BF16),
        scratch_shapes=[pltpu.VMEM((halo + tm + SUBLANES, cw), F32), pltpu.VMEM((tm, cw), F32)],
        compiler_params=pltpu.CompilerParams(
            dimension_semantics=("parallel", "arbitrary"), vmem_limit_bytes=VMEM_LIMIT),
    )(p, p, p, p, conv_w_pad, vecs)


def _out_kernel(ya_ref, yb_ref, wa_ref, wb_ref, x_ref, mod_ref, o_ref, *, gate_row):
    acc = jnp.dot(ya_ref[...], wa_ref[...], preferred_element_type=F32)
    acc = acc + jnp.dot(yb_ref[...], wb_ref[...], preferred_element_type=F32)
    o_ref[...] = x_ref[...] + mod_ref[gate_row:gate_row + 1, :] * acc


def _out_proj(ya, yb, w_out_bf, x2, mod3, seq, gate_row):
    m, rw = ya.shape
    cw = yb.shape[1]
    d = x2.shape[1]
    assert rw == cw
    tm = _tile(seq, 1024)
    tn = _tile(d, 512)
    per_b = seq // tm
    return pl.pallas_call(
        functools.partial(_out_kernel, gate_row=gate_row),
        name="out_proj",
        grid=(m // tm, d // tn),
        in_specs=[pl.BlockSpec((tm, rw), lambda i, j: (i, 0)),
                  pl.BlockSpec((tm, cw), lambda i, j: (i, 0)),
                  pl.BlockSpec((rw, tn), lambda i, j: (0, j)),
                  pl.BlockSpec((cw, tn), lambda i, j: (1, j)),
                  pl.BlockSpec((tm, tn), lambda i, j: (i, j)),
                  pl.BlockSpec((None, 8, tn), lambda i, j: (i // per_b, 0, j))],
        out_specs=pl.BlockSpec((tm, tn), lambda i, j: (i, j)),
        out_shape=jax.ShapeDtypeStruct((m, d), F32),
        compiler_params=pltpu.CompilerParams(dimension_semantics=("parallel", "parallel"),
                                             vmem_limit_bytes=VMEM_LIMIT),
    )(ya, yb, w_out_bf, w_out_bf, x2, mod3)


def _ffn_kernel(h_ref, w1_ref, w2_ref, x_ref, mod_ref, fg_ref, o_ref, *, gate_row, final_norm):
    j = pl.program_id(1)
    hid = jnp.dot(h_ref[...], w1_ref[...], preferred_element_type=F32)
    hid = jnp.maximum(hid, 0.0)
    hid = (hid * hid).astype(BF16)

    @pl.when(j == 0)
    def _():
        o_ref[...] = jnp.zeros_like(o_ref)

    d = o_ref.shape[1]
    tn = _tile(d, 1024)
    for n in range(d // tn):
        cols = slice(n * tn, (n + 1) * tn)
        o_ref[:, cols] += jnp.dot(hid, w2_ref[:, cols], preferred_element_type=F32)

    @pl.when(j == pl.num_programs(1) - 1)
    def _():
        y = x_ref[...] + mod_ref[gate_row:gate_row + 1, :] * o_ref[...]
        if final_norm:
            ms = jnp.mean(y * y, axis=-1, keepdims=True)
            y = y * lax.rsqrt(ms + RMS_EPS) * fg_ref[...]
        o_ref[...] = y


def _ffn(h, w1_bf, w2_bf, x1, mod3, final_g, seq, gate_row, final_norm):
    m, d = h.shape
    dff = w1_bf.shape[1]
    tm = _tile(seq, 512)
    tf = _tile(dff, 512)
    per_b = seq // tm
    return pl.pallas_call(
        functools.partial(_ffn_kernel, gate_row=gate_row, final_norm=final_norm),
        name="ffn",
        grid=(m // tm, dff // tf),
        in_specs=[pl.BlockSpec((tm, d), lambda i, j: (i, 0), pipeline_mode=pl.Buffered(1)),
                  pl.BlockSpec((d, tf), lambda i, j: (0, j)),
                  pl.BlockSpec((tf, d), lambda i, j: (j, 0)),
                  pl.BlockSpec((tm, d), lambda i, j: (i, 0), pipeline_mode=pl.Buffered(1)),
                  pl.BlockSpec((None, 8, d), lambda i, j: (i // per_b, 0, 0)),
                  pl.BlockSpec((1, d), lambda i, j: (0, 0))],
        out_specs=pl.BlockSpec((tm, d), lambda i, j: (i, 0)),
        out_shape=jax.ShapeDtypeStruct((m, d), F32),
        compiler_params=pltpu.CompilerParams(dimension_semantics=("parallel", "arbitrary"),
                                             vmem_limit_bytes=VMEM_LIMIT),
    )(h, w1_bf, w2_bf, x1, mod3, final_g.reshape(1, d))


def _rows8(rows, width):
    out = jnp.zeros((8, width), F32)
    for i, r in enumerate(rows):
        out = out.at[i].set(r.reshape(width).astype(F32))
    return out


def kernel(x, c, w_mod, b_mod, norm1_g, w_in, mu_shift, w0, w_decay_up, a0, w_a_up, w_g_up, k_k, k_a, r_k,
           lnx_g, lnx_b, conv_w, conv_b, ln_conv_g, ln_conv_b, w_out, norm2_g, w_ff1, w_ff2, final_g):
    bsz, seq, d = x.shape
    depth = w_mod.shape[0]
    rw = w0.shape[1]
    cw = conv_b.shape[1]
    n_dec, n_a, n_gate = w_decay_up.shape[1], w_a_up.shape[1], w_g_up.shape[1]
    n_lora = n_dec + n_a + n_gate
    lp = -(-n_lora // LANES) * LANES
    n_shift = 3 * rw + n_lora
    m = bsz * seq
    x2 = x.reshape(m, d)

    for l in range(depth):
        mod = _mod(c, w_mod[l], b_mod[l])
        mod3 = jnp.concatenate([mod.reshape(bsz, 6, d), jnp.zeros((bsz, 2, d), F32)], axis=1)

        w_in_p = jnp.concatenate(
            [w_in[l][:, n_shift:].astype(BF16), w_in[l][:, :n_shift].astype(BF16),
             jnp.zeros((d, lp - n_lora), BF16)], axis=1)
        mu = mu_shift[l]
        mus = _rows8([mu[:rw], mu[rw:2 * rw], mu[2 * rw:3 * rw]], rw)
        mul = jnp.zeros((1, lp), F32).at[0, :n_lora].set(mu[3 * rw:])
        vecs = _rows8([w0[l], a0[l], k_k[l], k_a[l], r_k[l], lnx_g[l], lnx_b[l]], rw)
        wd = jnp.zeros((lp, rw), F32).at[:n_dec].set(w_decay_up[l]).astype(BF16)
        wa = jnp.zeros((lp, rw), F32).at[n_dec:n_dec + n_a].set(w_a_up[l]).astype(BF16)
        wg = jnp.zeros((lp, rw), F32).at[n_dec + n_a:n_lora].set(w_g_up[l]).astype(BF16)
        conv_w_pad = jnp.zeros((32, cw), F32).at[:CONV_K].set(conv_w[l])
        conv_vecs = _rows8([conv_b[l], ln_conv_g[l], ln_conv_b[l]], cw)

        h1 = _norm_mod(x2, norm1_g[l], mod3, seq, shift_row=0, scale_row=1)
        p = _matmul(h1, w_in_p, 1024, 768)
        y_a = _rwkv(p, seq, rw, lp, 2 * cw, mus, mul, vecs, wd, wa, wg)
        y_b = _conv(p, seq, cw, 0, conv_w_pad, conv_vecs)
        x1 = _out_proj(y_a, y_b, w_out[l].astype(BF16), x2, mod3, seq, gate_row=2)
        h2 = _norm_mod(x1, norm2_g[l], mod3, seq, shift_row=3, scale_row=4)
        x2 = _ffn(h2, w_ff1[l].astype(BF16), w_ff2[l].astype(BF16), x1, mod3, final_g, seq, gate_row=5,
                  final_norm=(l == depth - 1))
    return x2.reshape(bsz, seq, d)
```

```python
import functools
import math

import jax
import jax.numpy as jnp
from jax import lax
from jax.experimental import pallas as pl
from jax.experimental.pallas import tpu as pltpu

F32 = jnp.float32
BF16 = jnp.bfloat16

HEAD_DIM = 64
HEAD_SHIFT = 6
CONV_K = 31
RMS_EPS = 1e-6
LN_EPS = 1e-5
GN_EPS = 64e-5
L2_EPS = 1e-12

LANES = 128
SUBLANES = 8
CONV_ROWS = 128
RWKV_PARTS = 2
FFN_EPILOGUE_ROWS = 64
PAIR = 2 * HEAD_DIM
CHUNK_SHIFT = 6
CHUNK = 64
VMEM_LIMIT = 58 * 1024 * 1024


def _tile(dim, pref, unit=LANES):
    t = min(dim, pref) // unit * unit
    while t > unit and dim % t:
        t -= unit
    assert t > 0 and dim % t == 0, (dim, pref)
    return t


def _bdot(a, b):
    return jnp.dot(a.astype(BF16), b.astype(BF16), preferred_element_type=F32)


def _split2(x):
    hi = x.astype(BF16)
    lo = (x - hi.astype(F32)).astype(BF16)
    return hi, lo


def _mod_kernel(c_ref, w_ref, b_ref, o_ref):
    c = c_ref[...]
    s = c * (1.0 / (1.0 + jnp.exp(-c)))
    s_hi, s_lo = _split2(s)
    w = w_ref[...]
    w_hi, w_lo = _split2(w)
    n = s.shape[0]
    top = jnp.dot(jnp.concatenate([s_hi, s_lo], axis=0), w_hi, preferred_element_type=F32)
    o_ref[...] = top[:n] + top[n:] + jnp.dot(s_hi, w_lo, preferred_element_type=F32) + b_ref[...]


def _mod(c, w_mod, b_mod):
    bsz, d = c.shape
    n = w_mod.shape[1]
    rows = 8
    c_pad = jnp.zeros((rows, d), F32).at[:bsz].set(c)
    tn = _tile(n, 512)
    out = pl.pallas_call(
        _mod_kernel,
        name="adaln_mod",
        grid=(n // tn,),
        in_specs=[pl.BlockSpec((rows, d), lambda j: (0, 0)),
                  pl.BlockSpec((d, tn), lambda j: (0, j)),
                  pl.BlockSpec((1, tn), lambda j: (0, j))],
        out_specs=pl.BlockSpec((rows, tn), lambda j: (0, j)),
        out_shape=jax.ShapeDtypeStruct((rows, n), F32),
        compiler_params=pltpu.CompilerParams(dimension_semantics=("parallel",),
                                             vmem_limit_bytes=VMEM_LIMIT),
    )(c_pad, w_mod, b_mod.reshape(1, n))
    return out[:bsz]


def _norm_mod_kernel(x_ref, g_ref, mod_ref, o_ref, *, shift_row, scale_row):
    x = x_ref[...]
    ms = jnp.mean(x * x, axis=-1, keepdims=True)
    y = x * lax.rsqrt(ms + RMS_EPS) * g_ref[...]
    sh = mod_ref[shift_row:shift_row + 1, :]
    sc = mod_ref[scale_row:scale_row + 1, :]
    o_ref[...] = (y * (1.0 + sc) + sh).astype(o_ref.dtype)


def _norm_mod(x2, g, mod3, seq, shift_row, scale_row):
    m, d = x2.shape
    tm = _tile(seq, 512)
    per_b = seq // tm
    return pl.pallas_call(
        functools.partial(_norm_mod_kernel, shift_row=shift_row, scale_row=scale_row),
        name="norm_mod",
        grid=(m // tm,),
        in_specs=[pl.BlockSpec((tm, d), lambda i: (i, 0)),
                  pl.BlockSpec((1, d), lambda i: (0, 0)),
                  pl.BlockSpec((None, 8, d), lambda i: (i // per_b, 0, 0))],
        out_specs=pl.BlockSpec((tm, d), lambda i: (i, 0)),
        out_shape=jax.ShapeDtypeStruct((m, d), BF16),
        compiler_params=pltpu.CompilerParams(dimension_semantics=("parallel",),
                                             vmem_limit_bytes=VMEM_LIMIT),
    )(x2, g.reshape(1, d), mod3)


def _matmul_kernel(a_ref, b_ref, o_ref):
    o_ref[...] = jnp.dot(a_ref[...], b_ref[...], preferred_element_type=F32)


def _matmul(a, b, tm_pref, tn_pref):
    m, k = a.shape
    n = b.shape[1]
    tm = _tile(m, tm_pref)
    tn = _tile(n, tn_pref)
    return pl.pallas_call(
        _matmul_kernel,
        name="in_proj",
        grid=(m // tm, n // tn),
        in_specs=[pl.BlockSpec((tm, k), lambda i, j: (i, 0)),
                  pl.BlockSpec((k, tn), lambda i, j: (0, j))],
        out_specs=pl.BlockSpec((tm, tn), lambda i, j: (i, j)),
        out_shape=jax.ShapeDtypeStruct((m, n), F32),
        compiler_params=pltpu.CompilerParams(dimension_semantics=("parallel", "parallel"),
                                             vmem_limit_bytes=VMEM_LIMIT),
    )(a, b)


def _shift_mix(x, halo_row, mu, is_first):
    prev = pltpu.roll(x, 1, axis=0)
    row = lax.broadcasted_iota(jnp.int32, x.shape, 0)
    first = jnp.where(is_first, jnp.zeros_like(halo_row), halo_row)
    prev = jnp.where(row == 0, first, prev)
    return x + (prev - x) * mu


def _segsum(x, ones_bd):
    hi, lo = _split2(x)
    n = x.shape[0]
    s = jnp.dot(jnp.concatenate([hi, lo], axis=0), ones_bd, preferred_element_type=F32)
    return s[:n] + s[n:]


def _bmm(a, b):
    return jnp.einsum("bmk,bkn->bmn", a.astype(BF16), b.astype(BF16), preferred_element_type=F32)


def _interleave(*streams):
    live = list(streams)
    while live:
        for it in list(live):
            try:
                next(it)
            except StopIteration:
                live.remove(it)


def _rwkv_kernel(pr_ref, pk_ref, pv_ref, pl_ref, hr_ref, hk_ref, hv_ref, hl_ref,
                 mus_ref, mul_ref, vec_ref, wd_ref, wa_ref, wg_ref, o_ref, z_s, *, n_parts):
    t_idx = pl.program_id(2)
    is_first = t_idx == 0
    tb, gl = pr_ref.shape
    n_pairs = gl // PAIR
    hb = tb // n_parts
    ncs = hb // CHUNK

    @pl.when(is_first)
    def _():
        z_s[...] = jnp.zeros_like(z_s)

    w0 = vec_ref[0:1, :]
    a0 = vec_ref[1:2, :]
    k_k = vec_ref[2:3, :]
    k_a = vec_ref[3:4, :]
    r_k = vec_ref[4:5, :]
    lnx_g = vec_ref[5:6, :]
    lnx_b = vec_ref[6:7, :]

    lrow = lax.broadcasted_iota(jnp.int32, (gl, gl), 0)
    lcol = lax.broadcasted_iota(jnp.int32, (gl, gl), 1)
    ones_g = jnp.where((lrow >> HEAD_SHIFT) == (lcol >> HEAD_SHIFT), 1.0, 0.0).astype(BF16)
    lane_lo = lax.broadcasted_iota(jnp.int32, (1, 1, PAIR), 2) < HEAD_DIM
    row = lax.broadcasted_iota(jnp.int32, (PAIR, PAIR), 0)
    col = lax.broadcasted_iota(jnp.int32, (PAIR, PAIR), 1)
    strict = col < row
    incl = col <= row
    diag = row == col
    tri = jnp.logical_and((row >> CHUNK_SHIFT) == (col >> CHUNK_SHIFT), incl)
    tri = jnp.broadcast_to(jnp.where(tri, 1.0, 0.0).astype(BF16), (hb // PAIR, PAIR, PAIR))
    n_dec = wd_ref.shape[0]
    n_a = wa_ref.shape[0]

    def chunks(x):
        return x.reshape(ncs, CHUNK, gl)

    def pairs(x):
        return jnp.concatenate([x[:, :, p * PAIR:(p + 1) * PAIR] for p in range(n_pairs)], axis=0)

    def stack(x):
        xp = pairs(x)
        return jnp.concatenate([jnp.where(lane_lo, xp, 0.0), jnp.where(lane_lo, 0.0, xp)], axis=1)

    zs = [z_s[p] for p in range(n_pairs)]
    part = [dict() for _ in range(n_parts)]

    def prep(h):
        d = part[h]
        rows = slice(h * hb, (h + 1) * hb)

        def shifted(x_ref, halo_ref, mu):
            if h == 0:
                return _shift_mix(x_ref[rows, :], halo_ref[7:8, :], mu, is_first)
            return _shift_mix(x_ref[rows, :], x_ref[h * hb - 1:h * hb, :], mu, False)

        r = shifted(pr_ref, hr_ref, mus_ref[0:1, :])
        yield
        k = shifted(pk_ref, hk_ref, mus_ref[1:2, :])
        yield
        v = shifted(pv_ref, hv_ref, mus_ref[2:3, :])
        yield
        lo = shifted(pl_ref, hl_ref, mul_ref[...])
        yield
        lo_w = lo[:, :n_dec]
        lo_a = lo[:, n_dec:n_dec + n_a]
        lo_g = lo[:, n_dec + n_a:]
        pre_w = w0 + _bdot(jnp.tanh(lo_w), wd_ref[...])
        lw = (-math.exp(-0.5)) / (1.0 + jnp.exp(-pre_w))
        yield
        a = 1.0 / (1.0 + jnp.exp(-(a0 + _bdot(lo_a, wa_ref[...]))))
        yield
        d["g"] = _bdot(1.0 / (1.0 + jnp.exp(-lo_g)), wg_ref[...])
        yield
        kk = k * k_k
        nrm = jnp.sqrt(_segsum(kk * kk, ones_g))
        kk = kk / jnp.maximum(nrm, L2_EPS)
        yield
        km = k * (1.0 + (a - 1.0) * k_a)
        d["bonus"] = _segsum(r * km * r_k, ones_g) * v
        b = kk * a
        yield
        hi = lw.astype(BF16)
        r1 = lw - hi.astype(F32)
        mid = r1.astype(BF16)
        lo3 = (r1 - mid.astype(F32)).astype(BF16)
        split = jnp.concatenate([hi, mid, lo3], axis=1).reshape(hb // PAIR, PAIR, 3 * gl)
        cs3 = jnp.einsum("bts,bsk->btk", tri, split, preferred_element_type=F32).reshape(hb, 3 * gl)
        cs = cs3[:, :gl] + cs3[:, gl:2 * gl] + cs3[:, 2 * gl:]
        yield
        cs_c = chunks(cs)
        cs_end = cs_c[:, CHUNK - 1:CHUNK, :]
        e_pos = jnp.exp(cs_c)
        e_neg = jnp.exp(-cs_c)
        e_end = jnp.exp(cs_end - cs_c)
        yield
        d["a_st"] = stack(-chunks(kk) * jnp.exp(cs_c - chunks(lw)))
        yield
        d["r_st"] = stack(chunks(r) * e_pos)
        yield
        d["v_st"] = stack(chunks(v)).astype(BF16)
        d["lhs"] = jnp.concatenate([d["a_st"], d["r_st"]], axis=1).astype(BF16)
        yield
        d["rhs"] = jnp.concatenate([stack(chunks(b) * e_neg), stack(chunks(km) * e_neg)], axis=1).astype(BF16)
        yield
        d["bh_st"] = stack(chunks(b) * e_end)
        yield
        d["kh_st"] = stack(chunks(km) * e_end)
        d["g_end"] = pairs(jnp.exp(cs_end))
        yield

    def solve(h):
        d = part[h]
        aa = jnp.einsum("bmk,bnk->bmn", d["lhs"], d["rhs"], preferred_element_type=F32)
        yield
        a_ab = jnp.where(strict, aa[:, :PAIR, :PAIR], 0.0)
        a_ak = jnp.where(strict, aa[:, :PAIR, PAIR:], 0.0).astype(BF16)
        a_rb = jnp.where(incl, aa[:, PAIR:, :PAIR], 0.0).astype(BF16)
        a_rk = jnp.where(incl, aa[:, PAIR:, PAIR:], 0.0).astype(BF16)
        yield
        eye = jnp.where(diag, 1.0, 0.0).astype(F32)
        blk = (row >> 3) == (col >> 3)
        a8 = jnp.where(blk, a_ab, 0.0)
        a2 = _bmm(a8, a8)
        yield
        a4 = _bmm(a2, a2)
        t = eye + a8
        yield
        t = t + _bmm(t, a2)
        yield
        t = t + _bmm(t, a4)
        yield
        for shift in (4, 5, 6):
            wider = (row >> shift) == (col >> shift)
            off = jnp.where(jnp.logical_and(wider, jnp.logical_not(blk)), a_ab, 0.0)
            ty = _bmm(t, off)
            yield
            t = t + _bmm(ty, t)
            yield
            blk = wider
        akv = jnp.einsum("bmk,bkn->bmn", a_ak, d["v_st"], preferred_element_type=F32)
        yield
        tx = _bmm(t, jnp.concatenate([d["a_st"], akv], axis=2)).astype(BF16)
        yield
        bh_t = jnp.swapaxes(d["bh_st"], 1, 2)
        kh_t = jnp.swapaxes(d["kh_st"], 1, 2)
        yield
        mn = _bmm(bh_t, tx)
        yield
        d["m_mat"] = (jnp.where(diag, d["g_end"], 0.0) + mn[:, :, :PAIR]).astype(BF16)
        d["n_mat"] = mn[:, :, PAIR:] + _bmm(kh_t, d["v_st"])
        yield
        pq = jnp.einsum("bmk,bkn->bmn", a_rb, tx, preferred_element_type=F32)
        yield
        d["p_st"] = (d["r_st"] + pq[:, :, :PAIR]).astype(BF16)
        d["q_st"] = pq[:, :, PAIR:] + jnp.einsum("bmk,bkn->bmn", a_rk, d["v_st"], preferred_element_type=F32)
        yield

    def chain(h):
        d = part[h]
        z_in = [[None] * ncs for _ in range(n_pairs)]
        for c in range(ncs):
            for p in range(n_pairs):
                i = p * ncs + c
                zb = zs[p].astype(BF16)
                z_in[p][c] = zb
                zs[p] = jnp.dot(d["m_mat"][i], zb, preferred_element_type=F32) + d["n_mat"][i]
            yield
        d["z_all"] = jnp.stack([z_in[p][c] for p in range(n_pairs) for c in range(ncs)], axis=0)
        yield

    def emit(h):
        d = part[h]
        o_st = jnp.einsum("bmk,bkn->bmn", d["p_st"], d["z_all"], preferred_element_type=F32) + d["q_st"]
        yield
        o_b = o_st[:, :CHUNK, :] + o_st[:, CHUNK:, :]
        o = jnp.concatenate([o_b[p * ncs:(p + 1) * ncs].reshape(hb, PAIR) for p in range(n_pairs)], axis=1)
        yield
        mean = _segsum(o, ones_g) * (1.0 / HEAD_DIM)
        yield
        dev = o - mean
        var = _segsum(dev * dev, ones_g) * (1.0 / HEAD_DIM)
        yield
        y = dev * lax.rsqrt(var + GN_EPS) * lnx_g + lnx_b
        o_ref[h * hb:(h + 1) * hb, :] = ((y + d["bonus"]) * d["g"]).astype(o_ref.dtype)
        d.clear()
        yield

    stages = (prep, solve, chain, emit)
    for slot in range(n_parts + len(stages) - 1):
        _interleave(*[stage(slot - s) for s, stage in enumerate(stages) if 0 <= slot - s < n_parts])

    for p in range(n_pairs):
        z_s[p] = zs[p]


def _rwkv(p, seq, rw, lp, col0, mus, mul, vecs, wd, wa, wg):
    m = p.shape[0]
    bsz = m // seq
    gl = _tile(rw, 256)
    tb = _tile(seq, 1024)
    n_t = seq // tb
    n_g = rw // gl
    halo = 8

    def blk(col0):
        return pl.BlockSpec((tb, gl), lambda b, g, t: (b * n_t + t, col0 + g))

    def halo_blk(col0):
        return pl.BlockSpec(
            (halo, gl), lambda b, g, t: (jnp.maximum((b * n_t + t) * (tb // halo) - 1, 0), col0 + g))

    assert col0 % gl == 0 and (col0 + 3 * rw) % lp == 0
    c_r = col0 // gl
    c_k = c_r + rw // gl
    c_v = c_k + rw // gl
    lora_col = (col0 + 3 * rw) // lp
    in_specs = [
        blk(c_r), blk(c_k), blk(c_v),
        pl.BlockSpec((tb, lp), lambda b, g, t: (b * n_t + t, lora_col)),
        halo_blk(c_r), halo_blk(c_k), halo_blk(c_v),
        pl.BlockSpec((halo, lp), lambda b, g, t: (jnp.maximum((b * n_t + t) * (tb // halo) - 1, 0), lora_col)),
        pl.BlockSpec((8, gl), lambda b, g, t: (0, g)),
        pl.BlockSpec((1, lp), lambda b, g, t: (0, 0)),
        pl.BlockSpec((8, gl), lambda b, g, t: (0, g)),
        pl.BlockSpec((wd.shape[0], gl), lambda b, g, t: (0, g)),
        pl.BlockSpec((wa.shape[0], gl), lambda b, g, t: (0, g)),
        pl.BlockSpec((wg.shape[0], gl), lambda b, g, t: (0, g)),
    ]
    assert wd.shape[0] + wa.shape[0] + wg.shape[0] == lp
    scratch = [pltpu.VMEM((gl // PAIR, PAIR, PAIR), F32)]
    return pl.pallas_call(
        functools.partial(_rwkv_kernel, n_parts=RWKV_PARTS if tb % (RWKV_PARTS * CHUNK) == 0 else 1),
        name="rwkv7_mix",
        grid=(bsz, n_g, n_t),
        in_specs=in_specs,
        out_specs=pl.BlockSpec((tb, gl), lambda b, g, t: (b * n_t + t, g)),
        out_shape=jax.ShapeDtypeStruct((m, rw), BF16),
        scratch_shapes=scratch,
        compiler_params=pltpu.CompilerParams(
            dimension_semantics=("parallel", "parallel", "arbitrary"), vmem_limit_bytes=VMEM_LIMIT),
    )(p, p, p, p, p, p, p, p, mus, mul, vecs, wd, wa, wg)


def _conv_kernel(a_ref, b_ref, ha_ref, hb_ref, w_ref, vec_ref, o_ref, h_s, c_s, *, halo):
    is_first = pl.program_id(1) == 0
    tm, cw = a_ref.shape
    rb = _tile(tm, CONV_ROWS, SUBLANES)

    def glu(a, b):
        return a * (1.0 / (1.0 + jnp.exp(-b)))

    h_prev = glu(ha_ref[...], hb_ref[...])
    h_s[0:halo, :] = jnp.where(is_first, jnp.zeros_like(h_prev), h_prev)
    h_s[halo:halo + tm, :] = glu(a_ref[...], b_ref[...])
    h_s[halo + tm:halo + tm + SUBLANES, :] = jnp.zeros((SUBLANES, cw), F32)

    first = halo - (CONV_K - 1)
    for s in range(cw // LANES):
        ls = slice(s * LANES, (s + 1) * LANES)
        for r0 in range(0, tm, rb):
            acc = None
            for q in range(SUBLANES):
                part = None
                for j in range(CONV_K):
                    if (first + j) % SUBLANES != q:
                        continue
                    base = r0 + first + j - q
                    term = h_s[base:base + rb + SUBLANES, ls] * w_ref[j:j + 1, ls]
                    part = term if part is None else part + term
                if part is None:
                    continue
                shifted = part[q:q + rb, :]
                acc = shifted if acc is None else acc + shifted
            c_s[r0:r0 + rb, ls] = acc + vec_ref[0:1, ls]

    acc = c_s[...]
    mu = jnp.mean(acc, axis=-1, keepdims=True)
    d = acc - mu
    var = jnp.mean(d * d, axis=-1, keepdims=True)
    y = d * lax.rsqrt(var + LN_EPS) * vec_ref[1:2, :] + vec_ref[2:3, :]
    o_ref[...] = (y * (1.0 / (1.0 + jnp.exp(-y)))).astype(o_ref.dtype)


def _conv(p, seq, cw, col0, conv_w_pad, vecs):
    m = p.shape[0]
    bsz = m // seq
    tm = _tile(seq, 256)
    n_t = seq // tm
    halo = 32
    ca = col0 // cw
    cb = ca + 1

    def blk(cblk):
        return pl.BlockSpec((tm, cw), lambda b, t: (b * n_t + t, cblk))

    def halo_blk(cblk):
        return pl.BlockSpec(
            (halo, cw), lambda b, t: (jnp.maximum((b * n_t + t) * (tm // halo) - 1, 0), cblk))

    return pl.pallas_call(
        functools.partial(_conv_kernel, halo=halo),
        name="conformer_conv",
        grid=(bsz, n_t),
        in_specs=[blk(ca), blk(cb), halo_blk(ca), halo_blk(cb),
                  pl.BlockSpec((32, cw), lambda b, t: (0, 0)),
                  pl.BlockSpec((8, cw), lambda b, t: (0, 0))],
        out_specs=pl.BlockSpec((tm, cw), lambda b, t: (b * n_t + t, 0)),
        out_shape=jax.ShapeDtypeStruct((m, cw), BF16),
        scratch_shapes=[pltpu.VMEM((halo + tm + SUBLANES, cw), F32), pltpu.VMEM((tm, cw), F32)],
        compiler_params=pltpu.CompilerParams(
            dimension_semantics=("parallel", "arbitrary"), vmem_limit_bytes=VMEM_LIMIT),
    )(p, p, p, p, conv_w_pad, vecs)


def _out_kernel(ya_ref, yb_ref, wa_ref, wb_ref, x_ref, mod_ref, o_ref, *, gate_row):
    acc = jnp.dot(ya_ref[...], wa_ref[...], preferred_element_type=F32)
    acc = acc + jnp.dot(yb_ref[...], wb_ref[...], preferred_element_type=F32)
    o_ref[...] = x_ref[...] + mod_ref[gate_row:gate_row + 1, :] * acc


def _out_proj(ya, yb, w_out_bf, x2, mod3, seq, gate_row):
    m, rw = ya.shape
    cw = yb.shape[1]
    d = x2.shape[1]
    assert rw == cw
    tm = _tile(seq, 1024)
    tn = _tile(d, 1024)
    per_b = seq // tm
    return pl.pallas_call(
        functools.partial(_out_kernel, gate_row=gate_row),
        name="out_proj",
        grid=(m // tm, d // tn),
        in_specs=[pl.BlockSpec((tm, rw), lambda i, j: (i, 0)),
                  pl.BlockSpec((tm, cw), lambda i, j: (i, 0)),
                  pl.BlockSpec((rw, tn), lambda i, j: (0, j)),
                  pl.BlockSpec((cw, tn), lambda i, j: (1, j)),
                  pl.BlockSpec((tm, tn), lambda i, j: (i, j)),
                  pl.BlockSpec((None, 8, tn), lambda i, j: (i // per_b, 0, j))],
        out_specs=pl.BlockSpec((tm, tn), lambda i, j: (i, j)),
        out_shape=jax.ShapeDtypeStruct((m, d), F32),
        compiler_params=pltpu.CompilerParams(dimension_semantics=("parallel", "parallel"),
                                             vmem_limit_bytes=VMEM_LIMIT),
    )(ya, yb, w_out_bf, w_out_bf, x2, mod3)


def _ffn_kernel(h_ref, mod_ref, fg_ref, w1_hbm, w2_hbm, x_hbm, o_ref, w1_buf, w2_buf, x_buf, w_sem, x_sem,
                *, gate_row, final_norm, tf):
    i = pl.program_id(0)
    n_i = pl.num_programs(0)
    tm, d = o_ref.shape
    n_j = w1_hbm.shape[1] // tf
    assert n_j % 2 == 0

    def w1_copy(j, slot):
        return pltpu.make_async_copy(w1_hbm.at[:, pl.ds(j * tf, tf)], w1_buf.at[slot], w_sem.at[0, slot])

    def w2_copy(j, slot):
        return pltpu.make_async_copy(w2_hbm.at[pl.ds(j * tf, tf), :], w2_buf.at[slot], w_sem.at[1, slot])

    x_copy = pltpu.make_async_copy(x_hbm.at[pl.ds(i * tm, tm), :], x_buf, x_sem.at[0])

    @pl.when(i == 0)
    def _():
        w1_copy(0, 0).start()
        w2_copy(0, 0).start()

    x_copy.start()
    o_ref[...] = jnp.zeros_like(o_ref)
    tn = _tile(d, 1024)

    def block(j, slot):
        w1_copy(j, slot).wait()
        w2_copy(j, slot).wait()
        nxt = j + 1

        @pl.when(nxt < n_j)
        def _():
            w1_copy(nxt, 1 - slot).start()
            w2_copy(nxt, 1 - slot).start()

        @pl.when(jnp.logical_and(nxt == n_j, i + 1 < n_i))
        def _():
            w1_copy(0, 1 - slot).start()
            w2_copy(0, 1 - slot).start()

        hid = jnp.dot(h_ref[...], w1_buf[slot], preferred_element_type=F32)
        hid = jnp.maximum(hid, 0.0)
        hid = (hid * hid).astype(BF16)
        for n in range(d // tn):
            cols = slice(n * tn, (n + 1) * tn)
            o_ref[:, cols] += jnp.dot(hid, w2_buf[slot, :, cols], preferred_element_type=F32)

    def two_blocks(jj, carry):
        block(2 * jj, 0)
        block(2 * jj + 1, 1)
        return carry

    lax.fori_loop(0, n_j // 2, two_blocks, 0)

    x_copy.wait()
    gate = mod_ref[gate_row:gate_row + 1, :]
    rt = _tile(tm, FFN_EPILOGUE_ROWS, SUBLANES)
    for r0 in range(0, tm, rt):
        rows = slice(r0, r0 + rt)
        y = x_buf[rows, :] + gate * o_ref[rows, :]
        if final_norm:
            ms = jnp.mean(y * y, axis=-1, keepdims=True)
            y = y * lax.rsqrt(ms + RMS_EPS) * fg_ref[...]
        o_ref[rows, :] = y


def _ffn(h, w1_bf, w2_bf, x1, mod3, final_g, seq, gate_row, final_norm):
    m, d = h.shape
    dff = w1_bf.shape[1]
    tm = _tile(seq, 512)
    tf = _tile(dff, 512)
    per_b = seq // tm
    return pl.pallas_call(
        functools.partial(_ffn_kernel, gate_row=gate_row, final_norm=final_norm, tf=tf),
        name="ffn",
        grid=(m // tm,),
        in_specs=[pl.BlockSpec((tm, d), lambda i: (i, 0)),
                  pl.BlockSpec((None, 8, d), lambda i: (i // per_b, 0, 0)),
                  pl.BlockSpec((1, d), lambda i: (0, 0)),
                  pl.BlockSpec(memory_space=pl.ANY),
                  pl.BlockSpec(memory_space=pl.ANY),
                  pl.BlockSpec(memory_space=pl.ANY)],
        out_specs=pl.BlockSpec((tm, d), lambda i: (i, 0)),
        out_shape=jax.ShapeDtypeStruct((m, d), F32),
        scratch_shapes=[pltpu.VMEM((2, d, tf), BF16), pltpu.VMEM((2, tf, d), BF16), pltpu.VMEM((tm, d), F32),
                        pltpu.SemaphoreType.DMA((2, 2)), pltpu.SemaphoreType.DMA((1,))],
        compiler_params=pltpu.CompilerParams(dimension_semantics=("arbitrary",),
                                             vmem_limit_bytes=VMEM_LIMIT),
    )(h, mod3, final_g.reshape(1, d), w1_bf, w2_bf, x1)


def _rows8(rows, width):
    out = jnp.zeros((8, width), F32)
    for i, r in enumerate(rows):
        out = out.at[i].set(r.reshape(width).astype(F32))
    return out


def kernel(x, c, w_mod, b_mod, norm1_g, w_in, mu_shift, w0, w_decay_up, a0, w_a_up, w_g_up, k_k, k_a, r_k,
           lnx_g, lnx_b, conv_w, conv_b, ln_conv_g, ln_conv_b, w_out, norm2_g, w_ff1, w_ff2, final_g):
    bsz, seq, d = x.shape
    depth = w_mod.shape[0]
    rw = w0.shape[1]
    cw = conv_b.shape[1]
    n_dec, n_a, n_gate = w_decay_up.shape[1], w_a_up.shape[1], w_g_up.shape[1]
    n_lora = n_dec + n_a + n_gate
    n_shift = 3 * rw + n_lora
    pads = [-(-n // LANES) * LANES for n in (n_dec, n_a, n_gate)]
    lp = sum(pads)
    m = bsz * seq
    x2 = x.reshape(m, d)

    def lane_padded(sections, rows, dtype):
        out = []
        for sec, width in zip(sections, pads):
            out.append(sec.astype(dtype))
            if width > sec.shape[1]:
                out.append(jnp.zeros((rows, width - sec.shape[1]), dtype))
        return out

    def row_padded(w, rows):
        return jnp.zeros((rows, w.shape[1]), F32).at[:w.shape[0]].set(w).astype(BF16)

    for l in range(depth):
        mod = _mod(c, w_mod[l], b_mod[l])
        mod3 = jnp.concatenate([mod.reshape(bsz, 6, d), jnp.zeros((bsz, 2, d), F32)], axis=1)

        bounds = (3 * rw, 3 * rw + n_dec, 3 * rw + n_dec + n_a, n_shift)
        w_lora = [w_in[l][:, lo:hi] for lo, hi in zip(bounds[:-1], bounds[1:])]
        w_in_p = jnp.concatenate(
            [w_in[l][:, n_shift:].astype(BF16), w_in[l][:, :3 * rw].astype(BF16)] + lane_padded(w_lora, d, BF16),
            axis=1)
        mu = mu_shift[l]
        mus = _rows8([mu[:rw], mu[rw:2 * rw], mu[2 * rw:3 * rw]], rw)
        mu_lora = [mu[None, lo:hi] for lo, hi in zip(bounds[:-1], bounds[1:])]
        mul = jnp.concatenate(lane_padded(mu_lora, 1, F32), axis=1)
        vecs = _rows8([w0[l], a0[l], k_k[l], k_a[l], r_k[l], lnx_g[l], lnx_b[l]], rw)
        wd = row_padded(w_decay_up[l], pads[0])
        wa = row_padded(w_a_up[l], pads[1])
        wg = row_padded(w_g_up[l], pads[2])
        conv_w_pad = jnp.zeros((32, cw), F32).at[:CONV_K].set(conv_w[l])
        conv_vecs = _rows8([conv_b[l], ln_conv_g[l], ln_conv_b[l]], cw)

        h1 = _norm_mod(x2, norm1_g[l], mod3, seq, shift_row=0, scale_row=1)
        p = _matmul(h1, w_in_p, 1024, 768)
        y_a = _rwkv(p, seq, rw, lp, 2 * cw, mus, mul, vecs, wd, wa, wg)
        y_b = _conv(p, seq, cw, 0, conv_w_pad, conv_vecs)
        x1 = _out_proj(y_a, y_b, w_out[l].astype(BF16), x2, mod3, seq, gate_row=2)
        h2 = _norm_mod(x1, norm2_g[l], mod3, seq, shift_row=3, scale_row=4)
        x2 = _ffn(h2, w_ff1[l].astype(BF16), w_ff2[l].astype(BF16), x1, mod3, final_g, seq, gate_row=5,
                  final_norm=(l == depth - 1))
    return x2.reshape(bsz, seq, d)
```

```python
import functools
import math

import jax
import jax.numpy as jnp
from jax import lax
from jax.experimental import pallas as pl
from jax.experimental.pallas import tpu as pltpu

F32 = jnp.float32
BF16 = jnp.bfloat16

HEAD_DIM = 64
HEAD_SHIFT = 6
CONV_K = 31
RMS_EPS = 1e-6
LN_EPS = 1e-5
GN_EPS = 64e-5
L2_EPS = 1e-12

LANES = 128
SUBLANES = 8
CONV_ROWS = 128
RWKV_PARTS = 4
FFN_EPILOGUE_ROWS = 64
PAIR = 2 * HEAD_DIM
CHUNK_SHIFT = 6
CHUNK = 64
VMEM_LIMIT = 58 * 1024 * 1024


def _tile(dim, pref, unit=LANES):
    t = min(dim, pref) // unit * unit
    while t > unit and dim % t:
        t -= unit
    assert t > 0 and dim % t == 0, (dim, pref)
    return t


def _bdot(a, b):
    return jnp.dot(a.astype(BF16), b.astype(BF16), preferred_element_type=F32)


def _split2(x):
    hi = x.astype(BF16)
    lo = (x - hi.astype(F32)).astype(BF16)
    return hi, lo


def _mod_kernel(c_ref, w_ref, b_ref, o_ref):
    c = c_ref[...]
    s = c * (1.0 / (1.0 + jnp.exp(-c)))
    s_hi, s_lo = _split2(s)
    w = w_ref[...]
    w_hi, w_lo = _split2(w)
    n = s.shape[0]
    top = jnp.dot(jnp.concatenate([s_hi, s_lo], axis=0), w_hi, preferred_element_type=F32)
    o_ref[...] = top[:n] + top[n:] + jnp.dot(s_hi, w_lo, preferred_element_type=F32) + b_ref[...]


def _mod(c, w_mod, b_mod):
    bsz, d = c.shape
    n = w_mod.shape[1]
    rows = 8
    c_pad = jnp.zeros((rows, d), F32).at[:bsz].set(c)
    tn = _tile(n, 512)
    out = pl.pallas_call(
        _mod_kernel,
        name="adaln_mod",
        grid=(n // tn,),
        in_specs=[pl.BlockSpec((rows, d), lambda j: (0, 0)),
                  pl.BlockSpec((d, tn), lambda j: (0, j)),
                  pl.BlockSpec((1, tn), lambda j: (0, j))],
        out_specs=pl.BlockSpec((rows, tn), lambda j: (0, j)),
        out_shape=jax.ShapeDtypeStruct((rows, n), F32),
        compiler_params=pltpu.CompilerParams(dimension_semantics=("parallel",),
                                             vmem_limit_bytes=VMEM_LIMIT),
    )(c_pad, w_mod, b_mod.reshape(1, n))
    return out[:bsz]


def _norm_mod_kernel(x_ref, g_ref, mod_ref, o_ref, *, shift_row, scale_row):
    x = x_ref[...]
    ms = jnp.mean(x * x, axis=-1, keepdims=True)
    y = x * lax.rsqrt(ms + RMS_EPS) * g_ref[...]
    sh = mod_ref[shift_row:shift_row + 1, :]
    sc = mod_ref[scale_row:scale_row + 1, :]
    o_ref[...] = (y * (1.0 + sc) + sh).astype(o_ref.dtype)


def _norm_mod(x2, g, mod3, seq, shift_row, scale_row):
    m, d = x2.shape
    tm = _tile(seq, 512)
    per_b = seq // tm
    return pl.pallas_call(
        functools.partial(_norm_mod_kernel, shift_row=shift_row, scale_row=scale_row),
        name="norm_mod",
        grid=(m // tm,),
        in_specs=[pl.BlockSpec((tm, d), lambda i: (i, 0)),
                  pl.BlockSpec((1, d), lambda i: (0, 0)),
                  pl.BlockSpec((None, 8, d), lambda i: (i // per_b, 0, 0))],
        out_specs=pl.BlockSpec((tm, d), lambda i: (i, 0)),
        out_shape=jax.ShapeDtypeStruct((m, d), BF16),
        compiler_params=pltpu.CompilerParams(dimension_semantics=("parallel",),
                                             vmem_limit_bytes=VMEM_LIMIT),
    )(x2, g.reshape(1, d), mod3)


def _matmul_kernel(a_ref, b_ref, o_ref):
    o_ref[...] = jnp.dot(a_ref[...], b_ref[...].astype(BF16), preferred_element_type=F32)


def _matmul(a, b, tm_pref, tn_pref):
    m, k = a.shape
    n = b.shape[1]
    tm = _tile(m, tm_pref)
    tn = _tile(n, tn_pref)
    return pl.pallas_call(
        _matmul_kernel,
        name="in_proj",
        grid=(m // tm, n // tn),
        in_specs=[pl.BlockSpec((tm, k), lambda i, j: (i, 0)),
                  pl.BlockSpec((k, tn), lambda i, j: (0, j))],
        out_specs=pl.BlockSpec((tm, tn), lambda i, j: (i, j)),
        out_shape=jax.ShapeDtypeStruct((m, n), F32),
        compiler_params=pltpu.CompilerParams(dimension_semantics=("parallel", "parallel"),
                                             vmem_limit_bytes=VMEM_LIMIT),
    )(a, b)


def _shift_mix(x, halo_row, mu, is_first):
    prev = pltpu.roll(x, 1, axis=0)
    row = lax.broadcasted_iota(jnp.int32, x.shape, 0)
    first = jnp.where(is_first, jnp.zeros_like(halo_row), halo_row)
    prev = jnp.where(row == 0, first, prev)
    return x + (prev - x) * mu


def _segsum(x, ones_bd):
    hi, lo = _split2(x)
    n = x.shape[0]
    s = jnp.dot(jnp.concatenate([hi, lo], axis=0), ones_bd, preferred_element_type=F32)
    return s[:n] + s[n:]


def _bmm(a, b):
    return jnp.einsum("bmk,bkn->bmn", a.astype(BF16), b.astype(BF16), preferred_element_type=F32)


def _interleave(*streams):
    live = list(streams)
    while live:
        for it in list(live):
            try:
                next(it)
            except StopIteration:
                live.remove(it)


def _rwkv_kernel(pr_ref, pk_ref, pv_ref, pl_ref, hr_ref, hk_ref, hv_ref, hl_ref,
                 mus_ref, mul_ref, vec_ref, wd_ref, wa_ref, wg_ref, o_ref, z_s, *, n_parts):
    t_idx = pl.program_id(2)
    is_first = t_idx == 0
    tb, gl = pr_ref.shape
    n_pairs = gl // PAIR
    hb = tb // n_parts
    ncs = hb // CHUNK

    @pl.when(is_first)
    def _():
        z_s[...] = jnp.zeros_like(z_s)

    w0 = vec_ref[0:1, :]
    a0 = vec_ref[1:2, :]
    k_k = vec_ref[2:3, :]
    k_a = vec_ref[3:4, :]
    r_k = vec_ref[4:5, :]
    lnx_g = vec_ref[5:6, :]
    lnx_b = vec_ref[6:7, :]

    lrow = lax.broadcasted_iota(jnp.int32, (gl, gl), 0)
    lcol = lax.broadcasted_iota(jnp.int32, (gl, gl), 1)
    ones_g = jnp.where((lrow >> HEAD_SHIFT) == (lcol >> HEAD_SHIFT), 1.0, 0.0).astype(BF16)
    lane_lo = lax.broadcasted_iota(jnp.int32, (1, 1, PAIR), 2) < HEAD_DIM
    row = lax.broadcasted_iota(jnp.int32, (PAIR, PAIR), 0)
    col = lax.broadcasted_iota(jnp.int32, (PAIR, PAIR), 1)
    strict = col < row
    incl = col <= row
    diag = row == col
    tri = jnp.logical_and((row >> CHUNK_SHIFT) == (col >> CHUNK_SHIFT), incl)
    tri = jnp.broadcast_to(jnp.where(tri, 1.0, 0.0).astype(BF16), (hb // PAIR, PAIR, PAIR))
    n_dec = wd_ref.shape[0]
    n_a = wa_ref.shape[0]

    def chunks(x):
        return x.reshape(ncs, CHUNK, gl)

    def pairs(x):
        return jnp.concatenate([x[:, :, p * PAIR:(p + 1) * PAIR] for p in range(n_pairs)], axis=0)

    def stack(x):
        xp = pairs(x)
        return jnp.concatenate([jnp.where(lane_lo, xp, 0.0), jnp.where(lane_lo, 0.0, xp)], axis=1)

    zs = [z_s[p] for p in range(n_pairs)]
    part = [dict() for _ in range(n_parts)]

    def prep(h):
        d = part[h]
        rows = slice(h * hb, (h + 1) * hb)

        def shifted(x_ref, halo_ref, mu):
            if h == 0:
                return _shift_mix(x_ref[rows, :], halo_ref[7:8, :], mu, is_first)
            return _shift_mix(x_ref[rows, :], x_ref[h * hb - 1:h * hb, :], mu, False)

        r = shifted(pr_ref, hr_ref, mus_ref[0:1, :])
        yield
        k = shifted(pk_ref, hk_ref, mus_ref[1:2, :])
        yield
        v = shifted(pv_ref, hv_ref, mus_ref[2:3, :])
        yield
        lo = shifted(pl_ref, hl_ref, mul_ref[...])
        yield
        lo_w = lo[:, :n_dec]
        lo_a = lo[:, n_dec:n_dec + n_a]
        lo_g = lo[:, n_dec + n_a:]
        pre_w = w0 + _bdot(jnp.tanh(lo_w), wd_ref[...])
        lw = (-math.exp(-0.5)) / (1.0 + jnp.exp(-pre_w))
        yield
        a = 1.0 / (1.0 + jnp.exp(-(a0 + _bdot(lo_a, wa_ref[...]))))
        yield
        d["g"] = _bdot(1.0 / (1.0 + jnp.exp(-lo_g)), wg_ref[...])
        yield
        kk = k * k_k
        nrm = jnp.sqrt(_segsum(kk * kk, ones_g))
        kk = kk / jnp.maximum(nrm, L2_EPS)
        yield
        km = k * (1.0 + (a - 1.0) * k_a)
        d["bonus"] = _segsum(r * km * r_k, ones_g) * v
        b = kk * a
        yield
        hi = lw.astype(BF16)
        r1 = lw - hi.astype(F32)
        mid = r1.astype(BF16)
        lo3 = (r1 - mid.astype(F32)).astype(BF16)
        split = jnp.concatenate([hi, mid, lo3], axis=1).reshape(hb // PAIR, PAIR, 3 * gl)
        cs3 = jnp.einsum("bts,bsk->btk", tri, split, preferred_element_type=F32).reshape(hb, 3 * gl)
        cs = cs3[:, :gl] + cs3[:, gl:2 * gl] + cs3[:, 2 * gl:]
        yield
        cs_c = chunks(cs)
        cs_end = cs_c[:, CHUNK - 1:CHUNK, :]
        e_pos = jnp.exp(cs_c)
        e_neg = jnp.exp(-cs_c)
        e_end = jnp.exp(cs_end - cs_c)
        yield
        d["a_st"] = stack(-chunks(kk) * jnp.exp(cs_c - chunks(lw)))
        yield
        d["r_st"] = stack(chunks(r) * e_pos)
        yield
        d["v_st"] = stack(chunks(v)).astype(BF16)
        d["lhs"] = jnp.concatenate([d["a_st"], d["r_st"]], axis=1).astype(BF16)
        yield
        d["rhs"] = jnp.concatenate([stack(chunks(b) * e_neg), stack(chunks(km) * e_neg)], axis=1).astype(BF16)
        yield
        d["bh_st"] = stack(chunks(b) * e_end)
        yield
        d["kh_st"] = stack(chunks(km) * e_end)
        d["g_end"] = pairs(jnp.exp(cs_end))
        yield

    def solve(h):
        d = part[h]
        aa = jnp.einsum("bmk,bnk->bmn", d["lhs"], d["rhs"], preferred_element_type=F32)
        yield
        a_ab = jnp.where(strict, aa[:, :PAIR, :PAIR], 0.0)
        a_ak = jnp.where(strict, aa[:, :PAIR, PAIR:], 0.0).astype(BF16)
        a_rb = jnp.where(incl, aa[:, PAIR:, :PAIR], 0.0).astype(BF16)
        a_rk = jnp.where(incl, aa[:, PAIR:, PAIR:], 0.0).astype(BF16)
        yield
        eye = jnp.where(diag, 1.0, 0.0).astype(F32)
        blk = (row >> 3) == (col >> 3)
        a8 = jnp.where(blk, a_ab, 0.0)
        a2 = _bmm(a8, a8)
        yield
        a4 = _bmm(a2, a2)
        t = eye + a8
        yield
        t = t + _bmm(t, a2)
        yield
        t = t + _bmm(t, a4)
        yield
        for shift in (4, 5, 6):
            wider = (row >> shift) == (col >> shift)
            off = jnp.where(jnp.logical_and(wider, jnp.logical_not(blk)), a_ab, 0.0)
            ty = _bmm(t, off)
            yield
            t = t + _bmm(ty, t)
            yield
            blk = wider
        akv = jnp.einsum("bmk,bkn->bmn", a_ak, d["v_st"], preferred_element_type=F32)
        yield
        tx = _bmm(t, jnp.concatenate([d["a_st"], akv], axis=2)).astype(BF16)
        yield
        bh_t = jnp.swapaxes(d["bh_st"], 1, 2)
        kh_t = jnp.swapaxes(d["kh_st"], 1, 2)
        yield
        mn = _bmm(bh_t, tx)
        yield
        d["m_mat"] = (jnp.where(diag, d["g_end"], 0.0) + mn[:, :, :PAIR]).astype(BF16)
        d["n_mat"] = mn[:, :, PAIR:] + _bmm(kh_t, d["v_st"])
        yield
        pq = jnp.einsum("bmk,bkn->bmn", a_rb, tx, preferred_element_type=F32)
        yield
        d["p_st"] = (d["r_st"] + pq[:, :, :PAIR]).astype(BF16)
        d["q_st"] = pq[:, :, PAIR:] + jnp.einsum("bmk,bkn->bmn", a_rk, d["v_st"], preferred_element_type=F32)
        yield

    def chain(h):
        d = part[h]
        z_in = [[None] * ncs for _ in range(n_pairs)]
        for c in range(ncs):
            for p in range(n_pairs):
                i = p * ncs + c
                zb = zs[p].astype(BF16)
                z_in[p][c] = zb
                zs[p] = jnp.dot(d["m_mat"][i], zb, preferred_element_type=F32) + d["n_mat"][i]
            yield
        d["z_all"] = jnp.stack([z_in[p][c] for p in range(n_pairs) for c in range(ncs)], axis=0)
        yield

    def emit(h):
        d = part[h]
        o_st = jnp.einsum("bmk,bkn->bmn", d["p_st"], d["z_all"], preferred_element_type=F32) + d["q_st"]
        yield
        o_b = o_st[:, :CHUNK, :] + o_st[:, CHUNK:, :]
        o = jnp.concatenate([o_b[p * ncs:(p + 1) * ncs].reshape(hb, PAIR) for p in range(n_pairs)], axis=1)
        yield
        mean = _segsum(o, ones_g) * (1.0 / HEAD_DIM)
        yield
        dev = o - mean
        var = _segsum(dev * dev, ones_g) * (1.0 / HEAD_DIM)
        yield
        y = dev * lax.rsqrt(var + GN_EPS) * lnx_g + lnx_b
        o_ref[h * hb:(h + 1) * hb, :] = ((y + d["bonus"]) * d["g"]).astype(o_ref.dtype)
        d.clear()
        yield

    stages = (prep, solve, chain, emit)
    for slot in range(n_parts + len(stages) - 1):
        _interleave(*[stage(slot - s) for s, stage in enumerate(stages) if 0 <= slot - s < n_parts])

    for p in range(n_pairs):
        z_s[p] = zs[p]


def _rwkv(p, seq, rw, lp, col0, mus, mul, vecs, wd, wa, wg):
    m = p.shape[0]
    bsz = m // seq
    gl = _tile(rw, 256)
    tb = _tile(seq, 2048)
    n_t = seq // tb
    n_g = rw // gl
    halo = 8

    def blk(col0):
        return pl.BlockSpec((tb, gl), lambda b, g, t: (b * n_t + t, col0 + g))

    def halo_blk(col0):
        return pl.BlockSpec(
            (halo, gl), lambda b, g, t: (jnp.maximum((b * n_t + t) * (tb // halo) - 1, 0), col0 + g))

    assert col0 % gl == 0 and (col0 + 3 * rw) % lp == 0
    c_r = col0 // gl
    c_k = c_r + rw // gl
    c_v = c_k + rw // gl
    lora_col = (col0 + 3 * rw) // lp
    in_specs = [
        blk(c_r), blk(c_k), blk(c_v),
        pl.BlockSpec((tb, lp), lambda b, g, t: (b * n_t + t, lora_col)),
        halo_blk(c_r), halo_blk(c_k), halo_blk(c_v),
        pl.BlockSpec((halo, lp), lambda b, g, t: (jnp.maximum((b * n_t + t) * (tb // halo) - 1, 0), lora_col)),
        pl.BlockSpec((8, gl), lambda b, g, t: (0, g)),
        pl.BlockSpec((1, lp), lambda b, g, t: (0, 0)),
        pl.BlockSpec((8, gl), lambda b, g, t: (0, g)),
        pl.BlockSpec((wd.shape[0], gl), lambda b, g, t: (0, g)),
        pl.BlockSpec((wa.shape[0], gl), lambda b, g, t: (0, g)),
        pl.BlockSpec((wg.shape[0], gl), lambda b, g, t: (0, g)),
    ]
    assert wd.shape[0] + wa.shape[0] + wg.shape[0] == lp
    scratch = [pltpu.VMEM((gl // PAIR, PAIR, PAIR), F32)]
    return pl.pallas_call(
        functools.partial(_rwkv_kernel, n_parts=RWKV_PARTS if tb % (RWKV_PARTS * CHUNK) == 0 else 1),
        name="rwkv7_mix",
        grid=(bsz, n_g, n_t),
        in_specs=in_specs,
        out_specs=pl.BlockSpec((tb, gl), lambda b, g, t: (b * n_t + t, g)),
        out_shape=jax.ShapeDtypeStruct((m, rw), BF16),
        scratch_shapes=scratch,
        compiler_params=pltpu.CompilerParams(
            dimension_semantics=("parallel", "parallel", "arbitrary"), vmem_limit_bytes=VMEM_LIMIT),
    )(p, p, p, p, p, p, p, p, mus, mul, vecs, wd, wa, wg)


def _conv_kernel(a_ref, b_ref, ha_ref, hb_ref, w_ref, vec_ref, o_ref, h_s, c_s, *, halo):
    is_first = pl.program_id(1) == 0
    tm, cw = a_ref.shape
    rb = _tile(tm, CONV_ROWS, SUBLANES)

    def glu(a, b):
        return a * (1.0 / (1.0 + jnp.exp(-b)))

    h_prev = glu(ha_ref[...], hb_ref[...])
    h_s[0:halo, :] = jnp.where(is_first, jnp.zeros_like(h_prev), h_prev)
    h_s[halo:halo + tm, :] = glu(a_ref[...], b_ref[...])
    h_s[halo + tm:halo + tm + SUBLANES, :] = jnp.zeros((SUBLANES, cw), F32)

    first = halo - (CONV_K - 1)
    for s in range(cw // LANES):
        ls = slice(s * LANES, (s + 1) * LANES)
        for r0 in range(0, tm, rb):
            acc = None
            for q in range(SUBLANES):
                part = None
                for j in range(CONV_K):
                    if (first + j) % SUBLANES != q:
                        continue
                    base = r0 + first + j - q
                    term = h_s[base:base + rb + SUBLANES, ls] * w_ref[j:j + 1, ls]
                    part = term if part is None else part + term
                if part is None:
                    continue
                shifted = part[q:q + rb, :]
                acc = shifted if acc is None else acc + shifted
            c_s[r0:r0 + rb, ls] = acc + vec_ref[0:1, ls]

    acc = c_s[...]
    mu = jnp.mean(acc, axis=-1, keepdims=True)
    d = acc - mu
    var = jnp.mean(d * d, axis=-1, keepdims=True)
    y = d * lax.rsqrt(var + LN_EPS) * vec_ref[1:2, :] + vec_ref[2:3, :]
    o_ref[...] = (y * (1.0 / (1.0 + jnp.exp(-y)))).astype(o_ref.dtype)


def _conv(p, seq, cw, col0, conv_w_pad, vecs):
    m = p.shape[0]
    bsz = m // seq
    tm = _tile(seq, 256)
    n_t = seq // tm
    halo = 32
    ca = col0 // cw
    cb = ca + 1

    def blk(cblk):
        return pl.BlockSpec((tm, cw), lambda b, t: (b * n_t + t, cblk))

    def halo_blk(cblk):
        return pl.BlockSpec(
            (halo, cw), lambda b, t: (jnp.maximum((b * n_t + t) * (tm // halo) - 1, 0), cblk))

    return pl.pallas_call(
        functools.partial(_conv_kernel, halo=halo),
        name="conformer_conv",
        grid=(bsz, n_t),
        in_specs=[blk(ca), blk(cb), halo_blk(ca), halo_blk(cb),
                  pl.BlockSpec((32, cw), lambda b, t: (0, 0)),
                  pl.BlockSpec((8, cw), lambda b, t: (0, 0))],
        out_specs=pl.BlockSpec((tm, cw), lambda b, t: (b * n_t + t, 0)),
        out_shape=jax.ShapeDtypeStruct((m, cw), BF16),
        scratch_shapes=[pltpu.VMEM((halo + tm + SUBLANES, cw), F32), pltpu.VMEM((tm, cw), F32)],
        compiler_params=pltpu.CompilerParams(
            dimension_semantics=("parallel", "arbitrary"), vmem_limit_bytes=VMEM_LIMIT),
    )(p, p, p, p, conv_w_pad, vecs)


def _out_kernel(ya_ref, yb_ref, wa_ref, wb_ref, x_ref, mod_ref, o_ref, *, gate_row):
    acc = jnp.dot(ya_ref[...], wa_ref[...], preferred_element_type=F32)
    acc = acc + jnp.dot(yb_ref[...], wb_ref[...], preferred_element_type=F32)
    o_ref[...] = x_ref[...] + mod_ref[gate_row:gate_row + 1, :] * acc


def _out_proj(ya, yb, w_out_bf, x2, mod3, seq, gate_row):
    m, rw = ya.shape
    cw = yb.shape[1]
    d = x2.shape[1]
    assert rw == cw
    tm = _tile(seq, 1024)
    tn = _tile(d, 1024)
    per_b = seq // tm
    return pl.pallas_call(
        functools.partial(_out_kernel, gate_row=gate_row),
        name="out_proj",
        grid=(m // tm, d // tn),
        in_specs=[pl.BlockSpec((tm, rw), lambda i, j: (i, 0)),
                  pl.BlockSpec((tm, cw), lambda i, j: (i, 0)),
                  pl.BlockSpec((rw, tn), lambda i, j: (0, j)),
                  pl.BlockSpec((cw, tn), lambda i, j: (1, j)),
                  pl.BlockSpec((tm, tn), lambda i, j: (i, j)),
                  pl.BlockSpec((None, 8, tn), lambda i, j: (i // per_b, 0, j))],
        out_specs=pl.BlockSpec((tm, tn), lambda i, j: (i, j)),
        out_shape=jax.ShapeDtypeStruct((m, d), F32),
        compiler_params=pltpu.CompilerParams(dimension_semantics=("parallel", "parallel"),
                                             vmem_limit_bytes=VMEM_LIMIT),
    )(ya, yb, w_out_bf, w_out_bf, x2, mod3)


def _ffn_kernel(h_ref, mod_ref, fg_ref, w1_hbm, w2_hbm, x_hbm, o_ref, w1_buf, w2_buf, x_buf, w_sem, x_sem,
                *, gate_row, final_norm, tf):
    i = pl.program_id(0)
    n_i = pl.num_programs(0)
    tm, d = o_ref.shape
    n_j = w1_hbm.shape[1] // tf
    assert n_j % 2 == 0

    def w1_copy(j, slot):
        return pltpu.make_async_copy(w1_hbm.at[:, pl.ds(j * tf, tf)], w1_buf.at[slot], w_sem.at[0, slot])

    def w2_copy(j, slot):
        return pltpu.make_async_copy(w2_hbm.at[pl.ds(j * tf, tf), :], w2_buf.at[slot], w_sem.at[1, slot])

    x_copy = pltpu.make_async_copy(x_hbm.at[pl.ds(i * tm, tm), :], x_buf, x_sem.at[0])

    @pl.when(i == 0)
    def _():
        w1_copy(0, 0).start()
        w2_copy(0, 0).start()

    x_copy.start()
    o_ref[...] = jnp.zeros_like(o_ref)
    tn = _tile(d, 1024)

    def block(j, slot):
        w1_copy(j, slot).wait()
        w2_copy(j, slot).wait()
        nxt = j + 1

        @pl.when(nxt < n_j)
        def _():
            w1_copy(nxt, 1 - slot).start()
            w2_copy(nxt, 1 - slot).start()

        @pl.when(jnp.logical_and(nxt == n_j, i + 1 < n_i))
        def _():
            w1_copy(0, 1 - slot).start()
            w2_copy(0, 1 - slot).start()

        hid = jnp.dot(h_ref[...], w1_buf[slot], preferred_element_type=F32)
        hid = jnp.maximum(hid, 0.0)
        hid = (hid * hid).astype(BF16)
        for n in range(d // tn):
            cols = slice(n * tn, (n + 1) * tn)
            o_ref[:, cols] += jnp.dot(hid, w2_buf[slot, :, cols], preferred_element_type=F32)

    def two_blocks(jj, carry):
        block(2 * jj, 0)
        block(2 * jj + 1, 1)
        return carry

    lax.fori_loop(0, n_j // 2, two_blocks, 0)

    x_copy.wait()
    gate = mod_ref[gate_row:gate_row + 1, :]
    rt = _tile(tm, FFN_EPILOGUE_ROWS, SUBLANES)
    for r0 in range(0, tm, rt):
        rows = slice(r0, r0 + rt)
        y = x_buf[rows, :] + gate * o_ref[rows, :]
        if final_norm:
            ms = jnp.mean(y * y, axis=-1, keepdims=True)
            y = y * lax.rsqrt(ms + RMS_EPS) * fg_ref[...]
        o_ref[rows, :] = y


def _ffn(h, w1_bf, w2_bf, x1, mod3, final_g, seq, gate_row, final_norm):
    m, d = h.shape
    dff = w1_bf.shape[1]
    tm = _tile(seq, 512)
    tf = _tile(dff, 512)
    per_b = seq // tm
    return pl.pallas_call(
        functools.partial(_ffn_kernel, gate_row=gate_row, final_norm=final_norm, tf=tf),
        name="ffn",
        grid=(m // tm,),
        in_specs=[pl.BlockSpec((tm, d), lambda i: (i, 0)),
                  pl.BlockSpec((None, 8, d), lambda i: (i // per_b, 0, 0)),
                  pl.BlockSpec((1, d), lambda i: (0, 0)),
                  pl.BlockSpec(memory_space=pl.ANY),
                  pl.BlockSpec(memory_space=pl.ANY),
                  pl.BlockSpec(memory_space=pl.ANY)],
        out_specs=pl.BlockSpec((tm, d), lambda i: (i, 0)),
        out_shape=jax.ShapeDtypeStruct((m, d), F32),
        scratch_shapes=[pltpu.VMEM((2, d, tf), BF16), pltpu.VMEM((2, tf, d), BF16), pltpu.VMEM((tm, d), F32),
                        pltpu.SemaphoreType.DMA((2, 2)), pltpu.SemaphoreType.DMA((1,))],
        compiler_params=pltpu.CompilerParams(dimension_semantics=("arbitrary",),
                                             vmem_limit_bytes=VMEM_LIMIT),
    )(h, mod3, final_g.reshape(1, d), w1_bf, w2_bf, x1)


def _rows8(rows, width):
    out = jnp.zeros((8, width), F32)
    for i, r in enumerate(rows):
        out = out.at[i].set(r.reshape(width).astype(F32))
    return out


def kernel(x, c, w_mod, b_mod, norm1_g, w_in, mu_shift, w0, w_decay_up, a0, w_a_up, w_g_up, k_k, k_a, r_k,
           lnx_g, lnx_b, conv_w, conv_b, ln_conv_g, ln_conv_b, w_out, norm2_g, w_ff1, w_ff2, final_g):
    bsz, seq, d = x.shape
    depth = w_mod.shape[0]
    rw = w0.shape[1]
    cw = conv_b.shape[1]
    n_dec, n_a, n_gate = w_decay_up.shape[1], w_a_up.shape[1], w_g_up.shape[1]
    n_lora = n_dec + n_a + n_gate
    n_shift = 3 * rw + n_lora
    pads = [-(-n // LANES) * LANES for n in (n_dec, n_a, n_gate)]
    lp = sum(pads)
    m = bsz * seq
    x2 = x.reshape(m, d)

    def lane_padded(sections, rows, dtype):
        out = []
        for sec, width in zip(sections, pads):
            out.append(sec.astype(dtype))
            if width > sec.shape[1]:
                out.append(jnp.zeros((rows, width - sec.shape[1]), dtype))
        return out

    def row_padded(w, rows):
        return jnp.zeros((rows, w.shape[1]), F32).at[:w.shape[0]].set(w).astype(BF16)

    for l in range(depth):
        mod = _mod(c, w_mod[l], b_mod[l])
        mod3 = jnp.concatenate([mod.reshape(bsz, 6, d), jnp.zeros((bsz, 2, d), F32)], axis=1)

        bounds = (3 * rw, 3 * rw + n_dec, 3 * rw + n_dec + n_a, n_shift)
        w_lora = [w_in[l][:, lo:hi] for lo, hi in zip(bounds[:-1], bounds[1:])]
        w_in_p = jnp.concatenate([w_in[l][:, n_shift:], w_in[l][:, :3 * rw]] + lane_padded(w_lora, d, F32), axis=1)
        mu = mu_shift[l]
        mus = _rows8([mu[:rw], mu[rw:2 * rw], mu[2 * rw:3 * rw]], rw)
        mu_lora = [mu[None, lo:hi] for lo, hi in zip(bounds[:-1], bounds[1:])]
        mul = jnp.concatenate(lane_padded(mu_lora, 1, F32), axis=1)
        vecs = _rows8([w0[l], a0[l], k_k[l], k_a[l], r_k[l], lnx_g[l], lnx_b[l]], rw)
        wd = row_padded(w_decay_up[l], pads[0])
        wa = row_padded(w_a_up[l], pads[1])
        wg = row_padded(w_g_up[l], pads[2])
        conv_w_pad = jnp.zeros((32, cw), F32).at[:CONV_K].set(conv_w[l])
        conv_vecs = _rows8([conv_b[l], ln_conv_g[l], ln_conv_b[l]], cw)

        h1 = _norm_mod(x2, norm1_g[l], mod3, seq, shift_row=0, scale_row=1)
        p = _matmul(h1, w_in_p, 1024, 768)
        y_a = _rwkv(p, seq, rw, lp, 2 * cw, mus, mul, vecs, wd, wa, wg)
        y_b = _conv(p, seq, cw, 0, conv_w_pad, conv_vecs)
        x1 = _out_proj(y_a, y_b, w_out[l].astype(BF16), x2, mod3, seq, gate_row=2)
        h2 = _norm_mod(x1, norm2_g[l], mod3, seq, shift_row=3, scale_row=4)
        x2 = _ffn(h2, w_ff1[l].astype(BF16), w_ff2[l].astype(BF16), x1, mod3, final_g, seq, gate_row=5,
                  final_norm=(l == depth - 1))
    return x2.reshape(bsz, seq, d)
```

```python
import functools
import math

import jax
import jax.numpy as jnp
from jax import lax
from jax.experimental import pallas as pl
from jax.experimental.pallas import tpu as pltpu

F32 = jnp.float32
BF16 = jnp.bfloat16

HEAD_DIM = 64
HEAD_SHIFT = 6
CONV_K = 31
RMS_EPS = 1e-6
LN_EPS = 1e-5
GN_EPS = 64e-5
L2_EPS = 1e-12

LANES = 128
SUBLANES = 8
CONV_ROWS = 128
RWKV_PARTS = 4
FFN_EPILOGUE_ROWS = 64
PAIR = 2 * HEAD_DIM
CHUNK_SHIFT = 6
CHUNK = 64
VMEM_LIMIT = 58 * 1024 * 1024


def _tile(dim, pref, unit=LANES):
    t = min(dim, pref) // unit * unit
    while t > unit and dim % t:
        t -= unit
    assert t > 0 and dim % t == 0, (dim, pref)
    return t


def _bdot(a, b):
    return jnp.dot(a.astype(BF16), b.astype(BF16), preferred_element_type=F32)


def _split2(x):
    hi = x.astype(BF16)
    lo = (x - hi.astype(F32)).astype(BF16)
    return hi, lo


def _mod_kernel(c_ref, w_ref, b_ref, o_ref):
    c = c_ref[...]
    s = c * (1.0 / (1.0 + jnp.exp(-c)))
    s_hi, s_lo = _split2(s)
    w = w_ref[...]
    w_hi, w_lo = _split2(w)
    n = s.shape[0]
    top = jnp.dot(jnp.concatenate([s_hi, s_lo], axis=0), w_hi, preferred_element_type=F32)
    o_ref[...] = top[:n] + top[n:] + jnp.dot(s_hi, w_lo, preferred_element_type=F32) + b_ref[...]


def _mod(c, w_mod, b_mod):
    bsz, d = c.shape
    n = w_mod.shape[1]
    rows = 8
    c_pad = jnp.zeros((rows, d), F32).at[:bsz].set(c)
    tn = _tile(n, 512)
    out = pl.pallas_call(
        _mod_kernel,
        name="adaln_mod",
        grid=(n // tn,),
        in_specs=[pl.BlockSpec((rows, d), lambda j: (0, 0)),
                  pl.BlockSpec((d, tn), lambda j: (0, j)),
                  pl.BlockSpec((1, tn), lambda j: (0, j))],
        out_specs=pl.BlockSpec((rows, tn), lambda j: (0, j)),
        out_shape=jax.ShapeDtypeStruct((rows, n), F32),
        compiler_params=pltpu.CompilerParams(dimension_semantics=("parallel",),
                                             vmem_limit_bytes=VMEM_LIMIT),
    )(c_pad, w_mod, b_mod.reshape(1, n))
    return out[:bsz]


def _norm_mod_kernel(x_ref, g_ref, mod_ref, o_ref, *, shift_row, scale_row):
    x = x_ref[...]
    ms = jnp.mean(x * x, axis=-1, keepdims=True)
    y = x * lax.rsqrt(ms + RMS_EPS) * g_ref[...]
    sh = mod_ref[shift_row:shift_row + 1, :]
    sc = mod_ref[scale_row:scale_row + 1, :]
    o_ref[...] = (y * (1.0 + sc) + sh).astype(o_ref.dtype)


def _norm_mod(x2, g, mod3, seq, shift_row, scale_row):
    m, d = x2.shape
    tm = _tile(seq, 512)
    per_b = seq // tm
    return pl.pallas_call(
        functools.partial(_norm_mod_kernel, shift_row=shift_row, scale_row=scale_row),
        name="norm_mod",
        grid=(m // tm,),
        in_specs=[pl.BlockSpec((tm, d), lambda i: (i, 0)),
                  pl.BlockSpec((1, d), lambda i: (0, 0)),
                  pl.BlockSpec((None, 8, d), lambda i: (i // per_b, 0, 0))],
        out_specs=pl.BlockSpec((tm, d), lambda i: (i, 0)),
        out_shape=jax.ShapeDtypeStruct((m, d), BF16),
        compiler_params=pltpu.CompilerParams(dimension_semantics=("parallel",),
                                             vmem_limit_bytes=VMEM_LIMIT),
    )(x2, g.reshape(1, d), mod3)


def _matmul_kernel(a_ref, b_ref, o_ref):
    o_ref[...] = jnp.dot(a_ref[...], b_ref[...], preferred_element_type=F32)


def _matmul(a, b, tm_pref, tn_pref):
    m, k = a.shape
    n = b.shape[1]
    tm = _tile(m, tm_pref)
    tn = _tile(n, tn_pref)
    return pl.pallas_call(
        _matmul_kernel,
        name="in_proj",
        grid=(m // tm, n // tn),
        in_specs=[pl.BlockSpec((tm, k), lambda i, j: (i, 0)),
                  pl.BlockSpec((k, tn), lambda i, j: (0, j))],
        out_specs=pl.BlockSpec((tm, tn), lambda i, j: (i, j)),
        out_shape=jax.ShapeDtypeStruct((m, n), F32),
        compiler_params=pltpu.CompilerParams(dimension_semantics=("parallel", "parallel"),
                                             vmem_limit_bytes=VMEM_LIMIT),
    )(a, b)


def _shift_mix(x, halo_row, mu, is_first):
    prev = pltpu.roll(x, 1, axis=0)
    row = lax.broadcasted_iota(jnp.int32, x.shape, 0)
    first = jnp.where(is_first, jnp.zeros_like(halo_row), halo_row)
    prev = jnp.where(row == 0, first, prev)
    return x + (prev - x) * mu


def _segsum(x, ones_bd):
    hi, lo = _split2(x)
    n = x.shape[0]
    s = jnp.dot(jnp.concatenate([hi, lo], axis=0), ones_bd, preferred_element_type=F32)
    return s[:n] + s[n:]


def _bmm(a, b):
    return jnp.einsum("bmk,bkn->bmn", a.astype(BF16), b.astype(BF16), preferred_element_type=F32)


def _interleave(*streams):
    live = list(streams)
    while live:
        for it in list(live):
            try:
                next(it)
            except StopIteration:
                live.remove(it)


def _rwkv_kernel(pr_ref, pk_ref, pv_ref, pl_ref, hr_ref, hk_ref, hv_ref, hl_ref,
                 mus_ref, mul_ref, vec_ref, wd_ref, wa_ref, wg_ref, o_ref, z_s, *, n_parts):
    t_idx = pl.program_id(2)
    is_first = t_idx == 0
    tb, gl = pr_ref.shape
    n_pairs = gl // PAIR
    hb = tb // n_parts
    ncs = hb // CHUNK

    @pl.when(is_first)
    def _():
        z_s[...] = jnp.zeros_like(z_s)

    w0 = vec_ref[0:1, :]
    a0 = vec_ref[1:2, :]
    k_k = vec_ref[2:3, :]
    k_a = vec_ref[3:4, :]
    r_k = vec_ref[4:5, :]
    lnx_g = vec_ref[5:6, :]
    lnx_b = vec_ref[6:7, :]

    lrow = lax.broadcasted_iota(jnp.int32, (gl, gl), 0)
    lcol = lax.broadcasted_iota(jnp.int32, (gl, gl), 1)
    ones_g = jnp.where((lrow >> HEAD_SHIFT) == (lcol >> HEAD_SHIFT), 1.0, 0.0).astype(BF16)
    lane_lo = lax.broadcasted_iota(jnp.int32, (1, 1, PAIR), 2) < HEAD_DIM
    row = lax.broadcasted_iota(jnp.int32, (PAIR, PAIR), 0)
    col = lax.broadcasted_iota(jnp.int32, (PAIR, PAIR), 1)
    strict = col < row
    incl = col <= row
    diag = row == col
    tri = jnp.logical_and((row >> CHUNK_SHIFT) == (col >> CHUNK_SHIFT), incl)
    tri = jnp.broadcast_to(jnp.where(tri, 1.0, 0.0).astype(BF16), (hb // PAIR, PAIR, PAIR))
    n_dec = wd_ref.shape[0]
    n_a = wa_ref.shape[0]

    def chunks(x):
        return x.reshape(ncs, CHUNK, gl)

    def pairs(x):
        return jnp.concatenate([x[:, :, p * PAIR:(p + 1) * PAIR] for p in range(n_pairs)], axis=0)

    def stack(x):
        xp = pairs(x)
        return jnp.concatenate([jnp.where(lane_lo, xp, 0.0), jnp.where(lane_lo, 0.0, xp)], axis=1)

    zs = [z_s[p] for p in range(n_pairs)]
    part = [dict() for _ in range(n_parts)]

    def prep(h):
        d = part[h]
        rows = slice(h * hb, (h + 1) * hb)

        def shifted(x_ref, halo_ref, mu):
            if h == 0:
                return _shift_mix(x_ref[rows, :], halo_ref[7:8, :], mu, is_first)
            return _shift_mix(x_ref[rows, :], x_ref[h * hb - 1:h * hb, :], mu, False)

        r = shifted(pr_ref, hr_ref, mus_ref[0:1, :])
        yield
        k = shifted(pk_ref, hk_ref, mus_ref[1:2, :])
        yield
        v = shifted(pv_ref, hv_ref, mus_ref[2:3, :])
        yield
        lo = shifted(pl_ref, hl_ref, mul_ref[...])
        yield
        lo_w = lo[:, :n_dec]
        lo_a = lo[:, n_dec:n_dec + n_a]
        lo_g = lo[:, n_dec + n_a:]
        pre_w = w0 + _bdot(jnp.tanh(lo_w), wd_ref[...])
        lw = (-math.exp(-0.5)) / (1.0 + jnp.exp(-pre_w))
        yield
        a = 1.0 / (1.0 + jnp.exp(-(a0 + _bdot(lo_a, wa_ref[...]))))
        yield
        d["g"] = _bdot(1.0 / (1.0 + jnp.exp(-lo_g)), wg_ref[...])
        yield
        kk = k * k_k
        nrm = jnp.sqrt(_segsum(kk * kk, ones_g))
        kk = kk / jnp.maximum(nrm, L2_EPS)
        yield
        km = k * (1.0 + (a - 1.0) * k_a)
        d["bonus"] = _segsum(r * km * r_k, ones_g) * v
        b = kk * a
        yield
        hi = lw.astype(BF16)
        r1 = lw - hi.astype(F32)
        mid = r1.astype(BF16)
        lo3 = (r1 - mid.astype(F32)).astype(BF16)
        split = jnp.concatenate([hi, mid, lo3], axis=1).reshape(hb // PAIR, PAIR, 3 * gl)
        cs3 = jnp.einsum("bts,bsk->btk", tri, split, preferred_element_type=F32).reshape(hb, 3 * gl)
        cs = cs3[:, :gl] + cs3[:, gl:2 * gl] + cs3[:, 2 * gl:]
        yield
        cs_c = chunks(cs)
        cs_end = cs_c[:, CHUNK - 1:CHUNK, :]
        e_pos = jnp.exp(cs_c)
        e_neg = jnp.exp(-cs_c)
        e_end = jnp.exp(cs_end - cs_c)
        yield
        d["a_st"] = stack(-chunks(kk) * jnp.exp(cs_c - chunks(lw)))
        yield
        d["r_st"] = stack(chunks(r) * e_pos)
        yield
        d["v_st"] = stack(chunks(v)).astype(BF16)
        d["lhs"] = jnp.concatenate([d["a_st"], d["r_st"]], axis=1).astype(BF16)
        yield
        d["rhs"] = jnp.concatenate([stack(chunks(b) * e_neg), stack(chunks(km) * e_neg)], axis=1).astype(BF16)
        yield
        d["bh_st"] = stack(chunks(b) * e_end)
        yield
        d["kh_st"] = stack(chunks(km) * e_end)
        d["g_end"] = pairs(jnp.exp(cs_end))
        yield

    def solve(h):
        d = part[h]
        aa = jnp.einsum("bmk,bnk->bmn", d["lhs"], d["rhs"], preferred_element_type=F32)
        yield
        a_ab = jnp.where(strict, aa[:, :PAIR, :PAIR], 0.0)
        a_ak = jnp.where(strict, aa[:, :PAIR, PAIR:], 0.0).astype(BF16)
        a_rb = jnp.where(incl, aa[:, PAIR:, :PAIR], 0.0).astype(BF16)
        a_rk = jnp.where(incl, aa[:, PAIR:, PAIR:], 0.0).astype(BF16)
        yield
        eye = jnp.where(diag, 1.0, 0.0).astype(F32)
        blk = (row >> 3) == (col >> 3)
        a8 = jnp.where(blk, a_ab, 0.0)
        a2 = _bmm(a8, a8)
        yield
        a4 = _bmm(a2, a2)
        t = eye + a8
        yield
        t = t + _bmm(t, a2)
        yield
        t = t + _bmm(t, a4)
        yield
        for shift in (4, 5, 6):
            wider = (row >> shift) == (col >> shift)
            off = jnp.where(jnp.logical_and(wider, jnp.logical_not(blk)), a_ab, 0.0)
            ty = _bmm(t, off)
            yield
            t = t + _bmm(ty, t)
            yield
            blk = wider
        akv = jnp.einsum("bmk,bkn->bmn", a_ak, d["v_st"], preferred_element_type=F32)
        yield
        tx = _bmm(t, jnp.concatenate([d["a_st"], akv], axis=2)).astype(BF16)
        yield
        bh_t = jnp.swapaxes(d["bh_st"], 1, 2)
        kh_t = jnp.swapaxes(d["kh_st"], 1, 2)
        yield
        mn = _bmm(bh_t, tx)
        yield
        d["m_mat"] = (jnp.where(diag, d["g_end"], 0.0) + mn[:, :, :PAIR]).astype(BF16)
        d["n_mat"] = mn[:, :, PAIR:] + _bmm(kh_t, d["v_st"])
        yield
        pq = jnp.einsum("bmk,bkn->bmn", a_rb, tx, preferred_element_type=F32)
        yield
        d["p_st"] = (d["r_st"] + pq[:, :, :PAIR]).astype(BF16)
        d["q_st"] = pq[:, :, PAIR:] + jnp.einsum("bmk,bkn->bmn", a_rk, d["v_st"], preferred_element_type=F32)
        yield

    def chain(h):
        d = part[h]
        z_in = [[None] * ncs for _ in range(n_pairs)]
        for c in range(ncs):
            for p in range(n_pairs):
                i = p * ncs + c
                zb = zs[p].astype(BF16)
                z_in[p][c] = zb
                zs[p] = jnp.dot(d["m_mat"][i], zb, preferred_element_type=F32) + d["n_mat"][i]
            yield
        d["z_all"] = jnp.stack([z_in[p][c] for p in range(n_pairs) for c in range(ncs)], axis=0)
        yield

    def emit(h):
        d = part[h]
        o_st = jnp.einsum("bmk,bkn->bmn", d["p_st"], d["z_all"], preferred_element_type=F32) + d["q_st"]
        yield
        o_b = o_st[:, :CHUNK, :] + o_st[:, CHUNK:, :]
        o = jnp.concatenate([o_b[p * ncs:(p + 1) * ncs].reshape(hb, PAIR) for p in range(n_pairs)], axis=1)
        yield
        mean = _segsum(o, ones_g) * (1.0 / HEAD_DIM)
        yield
        dev = o - mean
        var = _segsum(dev * dev, ones_g) * (1.0 / HEAD_DIM)
        yield
        y = dev * lax.rsqrt(var + GN_EPS) * lnx_g + lnx_b
        o_ref[h * hb:(h + 1) * hb, :] = ((y + d["bonus"]) * d["g"]).astype(o_ref.dtype)
        d.clear()
        yield

    stages = (prep, solve, chain, emit)
    for slot in range(n_parts + len(stages) - 1):
        _interleave(*[stage(slot - s) for s, stage in enumerate(stages) if 0 <= slot - s < n_parts])

    for p in range(n_pairs):
        z_s[p] = zs[p]


def _rwkv(p, seq, rw, lp, col0, mus, mul, vecs, wd, wa, wg):
    m = p.shape[0]
    bsz = m // seq
    gl = _tile(rw, 256)
    tb = _tile(seq, 2048)
    n_t = seq // tb
    n_g = rw // gl
    halo = 8

    def blk(col0):
        return pl.BlockSpec((tb, gl), lambda b, g, t: (b * n_t + t, col0 + g))

    def halo_blk(col0):
        return pl.BlockSpec(
            (halo, gl), lambda b, g, t: (jnp.maximum((b * n_t + t) * (tb // halo) - 1, 0), col0 + g))

    assert col0 % gl == 0 and (col0 + 3 * rw) % lp == 0
    c_r = col0 // gl
    c_k = c_r + rw // gl
    c_v = c_k + rw // gl
    lora_col = (col0 + 3 * rw) // lp
    in_specs = [
        blk(c_r), blk(c_k), blk(c_v),
        pl.BlockSpec((tb, lp), lambda b, g, t: (b * n_t + t, lora_col)),
        halo_blk(c_r), halo_blk(c_k), halo_blk(c_v),
        pl.BlockSpec((halo, lp), lambda b, g, t: (jnp.maximum((b * n_t + t) * (tb // halo) - 1, 0), lora_col)),
        pl.BlockSpec((8, gl), lambda b, g, t: (0, g)),
        pl.BlockSpec((1, lp), lambda b, g, t: (0, 0)),
        pl.BlockSpec((8, gl), lambda b, g, t: (0, g)),
        pl.BlockSpec((wd.shape[0], gl), lambda b, g, t: (0, g)),
        pl.BlockSpec((wa.shape[0], gl), lambda b, g, t: (0, g)),
        pl.BlockSpec((wg.shape[0], gl), lambda b, g, t: (0, g)),
    ]
    assert wd.shape[0] + wa.shape[0] + wg.shape[0] == lp
    scratch = [pltpu.VMEM((gl // PAIR, PAIR, PAIR), F32)]
    return pl.pallas_call(
        functools.partial(_rwkv_kernel, n_parts=RWKV_PARTS if tb % (RWKV_PARTS * CHUNK) == 0 else 1),
        name="rwkv7_mix",
        grid=(bsz, n_g, n_t),
        in_specs=in_specs,
        out_specs=pl.BlockSpec((tb, gl), lambda b, g, t: (b * n_t + t, g)),
        out_shape=jax.ShapeDtypeStruct((m, rw), BF16),
        scratch_shapes=scratch,
        compiler_params=pltpu.CompilerParams(
            dimension_semantics=("parallel", "parallel", "arbitrary"), vmem_limit_bytes=VMEM_LIMIT),
    )(p, p, p, p, p, p, p, p, mus, mul, vecs, wd, wa, wg)


def _conv_kernel(a_ref, b_ref, ha_ref, hb_ref, w_ref, vec_ref, o_ref, h_s, c_s, *, halo):
    is_first = pl.program_id(1) == 0
    tm, cw = a_ref.shape
    rb = _tile(tm, CONV_ROWS, SUBLANES)

    def glu(a, b):
        return a * (1.0 / (1.0 + jnp.exp(-b)))

    h_prev = glu(ha_ref[...], hb_ref[...])
    h_s[0:halo, :] = jnp.where(is_first, jnp.zeros_like(h_prev), h_prev)
    h_s[halo:halo + tm, :] = glu(a_ref[...], b_ref[...])
    h_s[halo + tm:halo + tm + SUBLANES, :] = jnp.zeros((SUBLANES, cw), F32)

    first = halo - (CONV_K - 1)
    for s in range(cw // LANES):
        ls = slice(s * LANES, (s + 1) * LANES)
        for r0 in range(0, tm, rb):
            acc = None
            for q in range(SUBLANES):
                part = None
                for j in range(CONV_K):
                    if (first + j) % SUBLANES != q:
                        continue
                    base = r0 + first + j - q
                    term = h_s[base:base + rb + SUBLANES, ls] * w_ref[j:j + 1, ls]
                    part = term if part is None else part + term
                if part is None:
                    continue
                shifted = part[q:q + rb, :]
                acc = shifted if acc is None else acc + shifted
            c_s[r0:r0 + rb, ls] = acc + vec_ref[0:1, ls]

    acc = c_s[...]
    mu = jnp.mean(acc, axis=-1, keepdims=True)
    d = acc - mu
    var = jnp.mean(d * d, axis=-1, keepdims=True)
    y = d * lax.rsqrt(var + LN_EPS) * vec_ref[1:2, :] + vec_ref[2:3, :]
    o_ref[...] = (y * (1.0 / (1.0 + jnp.exp(-y)))).astype(o_ref.dtype)


def _conv(p, seq, cw, col0, conv_w_pad, vecs):
    m = p.shape[0]
    bsz = m // seq
    tm = _tile(seq, 256)
    n_t = seq // tm
    halo = 32
    ca = col0 // cw
    cb = ca + 1

    def blk(cblk):
        return pl.BlockSpec((tm, cw), lambda b, t: (b * n_t + t, cblk))

    def halo_blk(cblk):
        return pl.BlockSpec(
            (halo, cw), lambda b, t: (jnp.maximum((b * n_t + t) * (tm // halo) - 1, 0), cblk))

    return pl.pallas_call(
        functools.partial(_conv_kernel, halo=halo),
        name="conformer_conv",
        grid=(bsz, n_t),
        in_specs=[blk(ca), blk(cb), halo_blk(ca), halo_blk(cb),
                  pl.BlockSpec((32, cw), lambda b, t: (0, 0)),
                  pl.BlockSpec((8, cw), lambda b, t: (0, 0))],
        out_specs=pl.BlockSpec((tm, cw), lambda b, t: (b * n_t + t, 0)),
        out_shape=jax.ShapeDtypeStruct((m, cw), BF16),
        scratch_shapes=[pltpu.VMEM((halo + tm + SUBLANES, cw), F32), pltpu.VMEM((tm, cw), F32)],
        compiler_params=pltpu.CompilerParams(
            dimension_semantics=("parallel", "arbitrary"), vmem_limit_bytes=VMEM_LIMIT),
    )(p, p, p, p, conv_w_pad, vecs)


def _out_kernel(ya_ref, yb_ref, wa_ref, wb_ref, x_ref, mod_ref, o_ref, *, gate_row):
    acc = jnp.dot(ya_ref[...], wa_ref[...], preferred_element_type=F32)
    acc = acc + jnp.dot(yb_ref[...], wb_ref[...], preferred_element_type=F32)
    o_ref[...] = x_ref[...] + mod_ref[gate_row:gate_row + 1, :] * acc


def _out_proj(ya, yb, w_out_bf, x2, mod3, seq, gate_row):
    m, rw = ya.shape
    cw = yb.shape[1]
    d = x2.shape[1]
    assert rw == cw
    tm = _tile(seq, 1024)
    tn = _tile(d, 1024)
    per_b = seq // tm
    return pl.pallas_call(
        functools.partial(_out_kernel, gate_row=gate_row),
        name="out_proj",
        grid=(m // tm, d // tn),
        in_specs=[pl.BlockSpec((tm, rw), lambda i, j: (i, 0)),
                  pl.BlockSpec((tm, cw), lambda i, j: (i, 0)),
                  pl.BlockSpec((rw, tn), lambda i, j: (0, j)),
                  pl.BlockSpec((cw, tn), lambda i, j: (1, j)),
                  pl.BlockSpec((tm, tn), lambda i, j: (i, j)),
                  pl.BlockSpec((None, 8, tn), lambda i, j: (i // per_b, 0, j))],
        out_specs=pl.BlockSpec((tm, tn), lambda i, j: (i, j)),
        out_shape=jax.ShapeDtypeStruct((m, d), F32),
        compiler_params=pltpu.CompilerParams(dimension_semantics=("parallel", "parallel"),
                                             vmem_limit_bytes=VMEM_LIMIT),
    )(ya, yb, w_out_bf, w_out_bf, x2, mod3)


def _ffn_kernel(h_ref, mod_ref, fg_ref, w1_hbm, w2_hbm, x_hbm, o_ref, w1_buf, w2_buf, x_buf, w_sem, x_sem,
                *, gate_row, final_norm, tf):
    i = pl.program_id(0)
    n_i = pl.num_programs(0)
    tm, d = o_ref.shape
    n_j = w1_hbm.shape[1] // tf
    assert n_j % 2 == 0

    def w1_copy(j, slot):
        return pltpu.make_async_copy(w1_hbm.at[:, pl.ds(j * tf, tf)], w1_buf.at[slot], w_sem.at[0, slot])

    def w2_copy(j, slot):
        return pltpu.make_async_copy(w2_hbm.at[pl.ds(j * tf, tf), :], w2_buf.at[slot], w_sem.at[1, slot])

    x_copy = pltpu.make_async_copy(x_hbm.at[pl.ds(i * tm, tm), :], x_buf, x_sem.at[0])

    @pl.when(i == 0)
    def _():
        w1_copy(0, 0).start()
        w2_copy(0, 0).start()

    x_copy.start()
    o_ref[...] = jnp.zeros_like(o_ref)
    tn = _tile(d, 1024)

    def block(j, slot):
        w1_copy(j, slot).wait()
        w2_copy(j, slot).wait()
        nxt = j + 1

        @pl.when(nxt < n_j)
        def _():
            w1_copy(nxt, 1 - slot).start()
            w2_copy(nxt, 1 - slot).start()

        @pl.when(jnp.logical_and(nxt == n_j, i + 1 < n_i))
        def _():
            w1_copy(0, 1 - slot).start()
            w2_copy(0, 1 - slot).start()

        hid = jnp.dot(h_ref[...], w1_buf[slot], preferred_element_type=F32)
        hid = jnp.maximum(hid, 0.0)
        hid = (hid * hid).astype(BF16)
        for n in range(d // tn):
            cols = slice(n * tn, (n + 1) * tn)
            o_ref[:, cols] += jnp.dot(hid, w2_buf[slot, :, cols], preferred_element_type=F32)

    def two_blocks(jj, carry):
        block(2 * jj, 0)
        block(2 * jj + 1, 1)
        return carry

    lax.fori_loop(0, n_j // 2, two_blocks, 0)

    x_copy.wait()
    gate = mod_ref[gate_row:gate_row + 1, :]
    rt = _tile(tm, FFN_EPILOGUE_ROWS, SUBLANES)
    for r0 in range(0, tm, rt):
        rows = slice(r0, r0 + rt)
        y = x_buf[rows, :] + gate * o_ref[rows, :]
        if final_norm:
            ms = jnp.mean(y * y, axis=-1, keepdims=True)
            y = y * lax.rsqrt(ms + RMS_EPS) * fg_ref[...]
        o_ref[rows, :] = y


def _ffn(h, w1_bf, w2_bf, x1, mod3, final_g, seq, gate_row, final_norm):
    m, d = h.shape
    dff = w1_bf.shape[1]
    tm = _tile(seq, 512)
    tf = _tile(dff, 512)
    per_b = seq // tm
    return pl.pallas_call(
        functools.partial(_ffn_kernel, gate_row=gate_row, final_norm=final_norm, tf=tf),
        name="ffn",
        grid=(m // tm,),
        in_specs=[pl.BlockSpec((tm, d), lambda i: (i, 0)),
                  pl.BlockSpec((None, 8, d), lambda i: (i // per_b, 0, 0)),
                  pl.BlockSpec((1, d), lambda i: (0, 0)),
                  pl.BlockSpec(memory_space=pl.ANY),
                  pl.BlockSpec(memory_space=pl.ANY),
                  pl.BlockSpec(memory_space=pl.ANY)],
        out_specs=pl.BlockSpec((tm, d), lambda i: (i, 0)),
        out_shape=jax.ShapeDtypeStruct((m, d), F32),
        scratch_shapes=[pltpu.VMEM((2, d, tf), BF16), pltpu.VMEM((2, tf, d), BF16), pltpu.VMEM((tm, d), F32),
                        pltpu.SemaphoreType.DMA((2, 2)), pltpu.SemaphoreType.DMA((1,))],
        compiler_params=pltpu.CompilerParams(dimension_semantics=("arbitrary",),
                                             vmem_limit_bytes=VMEM_LIMIT),
    )(h, mod3, final_g.reshape(1, d), w1_bf, w2_bf, x1)


def _rows8(rows, width):
    out = jnp.zeros((8, width), F32)
    for i, r in enumerate(rows):
        out = out.at[i].set(r.reshape(width).astype(F32))
    return out


def kernel(x, c, w_mod, b_mod, norm1_g, w_in, mu_shift, w0, w_decay_up, a0, w_a_up, w_g_up, k_k, k_a, r_k,
           lnx_g, lnx_b, conv_w, conv_b, ln_conv_g, ln_conv_b, w_out, norm2_g, w_ff1, w_ff2, final_g):
    bsz, seq, d = x.shape
    depth = w_mod.shape[0]
    rw = w0.shape[1]
    cw = conv_b.shape[1]
    n_dec, n_a, n_gate = w_decay_up.shape[1], w_a_up.shape[1], w_g_up.shape[1]
    n_lora = n_dec + n_a + n_gate
    n_shift = 3 * rw + n_lora
    pads = [-(-n // LANES) * LANES for n in (n_dec, n_a, n_gate)]
    lp = sum(pads)
    m = bsz * seq
    x2 = x.reshape(m, d)

    def lane_padded(sections, rows, dtype):
        out = []
        for sec, width in zip(sections, pads):
            out.append(sec.astype(dtype))
            if width > sec.shape[1]:
                out.append(jnp.zeros((rows, width - sec.shape[1]), dtype))
        return out

    def row_padded(w, rows):
        return jnp.zeros((rows, w.shape[1]), F32).at[:w.shape[0]].set(w).astype(BF16)

    for l in range(depth):
        mod = _mod(c, w_mod[l], b_mod[l])
        mod3 = jnp.concatenate([mod.reshape(bsz, 6, d), jnp.zeros((bsz, 2, d), F32)], axis=1)

        bounds = (3 * rw, 3 * rw + n_dec, 3 * rw + n_dec + n_a, n_shift)
        w_lora = [w_in[l][:, lo:hi] for lo, hi in zip(bounds[:-1], bounds[1:])]
        w_in_p = jnp.concatenate(
            [w_in[l][:, n_shift:].astype(BF16), w_in[l][:, :3 * rw].astype(BF16)] + lane_padded(w_lora, d, BF16),
            axis=1)
        mu = mu_shift[l]
        mus = _rows8([mu[:rw], mu[rw:2 * rw], mu[2 * rw:3 * rw]], rw)
        mu_lora = [mu[None, lo:hi] for lo, hi in zip(bounds[:-1], bounds[1:])]
        mul = jnp.concatenate(lane_padded(mu_lora, 1, F32), axis=1)
        vecs = _rows8([w0[l], a0[l], k_k[l], k_a[l], r_k[l], lnx_g[l], lnx_b[l]], rw)
        wd = row_padded(w_decay_up[l], pads[0])
        wa = row_padded(w_a_up[l], pads[1])
        wg = row_padded(w_g_up[l], pads[2])
        conv_w_pad = jnp.zeros((32, cw), F32).at[:CONV_K].set(conv_w[l])
        conv_vecs = _rows8([conv_b[l], ln_conv_g[l], ln_conv_b[l]], cw)

        h1 = _norm_mod(x2, norm1_g[l], mod3, seq, shift_row=0, scale_row=1)
        p = _matmul(h1, w_in_p, 1024, 768)
        y_a = _rwkv(p, seq, rw, lp, 2 * cw, mus, mul, vecs, wd, wa, wg)
        y_b = _conv(p, seq, cw, 0, conv_w_pad, conv_vecs)
        x1 = _out_proj(y_a, y_b, w_out[l].astype(BF16), x2, mod3, seq, gate_row=2)
        h2 = _norm_mod(x1, norm2_g[l], mod3, seq, shift_row=3, scale_row=4)
        x2 = _ffn(h2, w_ff1[l].astype(BF16), w_ff2[l].astype(BF16), x1, mod3, final_g, seq, gate_row=5,
                  final_norm=(l == depth - 1))
    return x2.reshape(bsz, seq, d)
```
